```python
import jax
import jax.numpy as jnp
from jax import lax
import numpy as np

D_MODEL = 1024
BATCH = 8
SEQ = 2048
DEPTH = 4
DEC_BATCH = 128
DEC_SEQ = 8
PAST_LEN = 2048
PAGE_SIZE = 128

N_META = 16
SB_HEADS = 16
SB_HEAD_DIM = D_MODEL // SB_HEADS
SB_Q_BLOCK = 128
SB_LOGIT_BIAS_INIT = -8.0
GLA_HEADS = 4
GLA_KEY_DIM = D_MODEL // 2
GLA_VALUE_DIM = D_MODEL
GLA_HEAD_K = GLA_KEY_DIM // GLA_HEADS
GLA_HEAD_V = GLA_VALUE_DIM // GLA_HEADS
GLA_GATE_RANK = 16
GLA_TAU = 16.0
GLA_CHUNK = 64
D_FF = ((8 * D_MODEL // 3 + 127) // 128) * 128
N_EXPERTS = 8
TOP_K = 2
D_FF_EXPERT = 7 * D_MODEL // 2
DEEPNORM_ALPHA = float((2 * DEPTH) ** 0.25)
DEEPNORM_BETA = float((8 * DEPTH) ** -0.25)
LN_EPS = 1e-5
RMS_EPS = 1e-6

kernel_name = 'stickbreak_gla_deepnorm_moe_decoder_step'

F32 = jnp.float32


def layer_norm(x, g, b):
    xf = x.astype(F32)
    mu = jnp.mean(xf, axis=-1, keepdims=True)
    xc = xf - mu
    var = jnp.mean(xc * xc, axis=-1, keepdims=True)
    return (xc * lax.rsqrt(var + LN_EPS) * g.astype(F32) + b.astype(F32)).astype(x.dtype)


def deepnorm(x, y, g, b):
    return layer_norm(DEEPNORM_ALPHA * x + y, g, b)


def sb_attend(q, k, v, q_pos, k_pos, sb_bias):
    z = jnp.einsum('bqhd,bkhd->bhqk', q, k).astype(F32) * (SB_HEAD_DIM ** -0.5) + sb_bias.astype(F32)[None, :, None, None]
    causal = k_pos[None, :] < q_pos[:, None]
    log_beta = jax.nn.log_sigmoid(z)
    log_keep = jnp.where(causal, jax.nn.log_sigmoid(-z), 0.0)
    log_stick = lax.cumsum(log_keep, axis=3, reverse=True) - log_keep
    w = jnp.where(causal, jnp.exp(log_beta + log_stick), 0.0)
    return jnp.einsum('bhqk,bkhd->bqhd', w.astype(v.dtype), v)


def sb_project(h, w_qkv):
    B, T, _ = h.shape
    qkv = (h @ w_qkv).reshape(B, T, 3, SB_HEADS, SB_HEAD_DIM)
    return qkv[:, :, 0], qkv[:, :, 1], qkv[:, :, 2]


def sb_prompt(h, w_qkv, w_o, sb_bias):
    B, L, _ = h.shape
    q, k, v = sb_project(h, w_qkv)
    pos = jnp.arange(L, dtype=jnp.int32)
    o_meta = sb_attend(q[:, :N_META], k[:, :N_META], v[:, :N_META], pos[:N_META], pos[:N_META], sb_bias)
    n_blk = (L - N_META) // SB_Q_BLOCK
    q_blk = jnp.moveaxis(q[:, N_META:].reshape(B, n_blk, SB_Q_BLOCK, SB_HEADS, SB_HEAD_DIM), 1, 0)
    pos_blk = pos[N_META:].reshape(n_blk, SB_Q_BLOCK)
    o_blk = lax.map(lambda qp: sb_attend(qp[0], k, v, qp[1], pos, sb_bias), (q_blk, pos_blk))
    o_real = jnp.moveaxis(o_blk, 0, 1).reshape(B, L - N_META, SB_HEADS, SB_HEAD_DIM)
    o = jnp.concatenate([o_meta, o_real], axis=1).reshape(B, L, D_MODEL)
    return o @ w_o, k, v


def sb_sample(h, cache_k, cache_v, page_table, w_qkv, w_o, sb_bias):
    B, T, _ = h.shape
    q, k, v = sb_project(h, w_qkv)
    n_past = page_table.shape[1] * cache_k.shape[1]
    past_k = cache_k[page_table].reshape(B, n_past, SB_HEADS, SB_HEAD_DIM)
    past_v = cache_v[page_table].reshape(B, n_past, SB_HEADS, SB_HEAD_DIM)
    k_all = jnp.concatenate([past_k, k.astype(past_k.dtype)], axis=1)
    v_all = jnp.concatenate([past_v, v.astype(past_v.dtype)], axis=1)
    q_pos = n_past + jnp.arange(T, dtype=jnp.int32)
    k_pos = jnp.arange(n_past + T, dtype=jnp.int32)
    o = sb_attend(q, k_all, v_all, q_pos, k_pos, sb_bias).reshape(B, T, D_MODEL)
    return o @ w_o, k, v


def gla_project(h, w_in, w_gate2, b_gate):
    B, T, _ = h.shape
    proj = h @ w_in
    cuts = [GLA_KEY_DIM, 2 * GLA_KEY_DIM, 2 * GLA_KEY_DIM + GLA_VALUE_DIM, 2 * GLA_KEY_DIM + 2 * GLA_VALUE_DIM]
    q, k, v, r, g_low = jnp.split(proj, cuts, axis=-1)
    log_gate = jax.nn.log_sigmoid((g_low @ w_gate2 + b_gate).astype(F32)) / GLA_TAU
    q = q.astype(F32).reshape(B, T, GLA_HEADS, GLA_HEAD_K) * (GLA_HEAD_K ** -0.5)
    k = k.astype(F32).reshape(B, T, GLA_HEADS, GLA_HEAD_K)
    v = v.astype(F32).reshape(B, T, GLA_HEADS, GLA_HEAD_V)
    log_gate = log_gate.reshape(B, T, GLA_HEADS, GLA_HEAD_K)
    return q, k, v, log_gate, r


def gla_chunk(S, q, k, v, log_gate):
    C = q.shape[1]
    b = jnp.cumsum(log_gate, axis=1)
    mask = jnp.tril(jnp.ones((C, C), dtype=bool))
    diff = b[:, :, None] - b[:, None, :]
    decay = jnp.exp(jnp.where(mask[None, :, :, None, None], diff, -jnp.inf))
    scores = jnp.einsum('btshd,bshd->bths', q[:, :, None] * decay, k)
    o_intra = jnp.einsum('bths,bshv->bthv', scores, v)
    o_inter = jnp.einsum('bthd,bhdv->bthv', q * jnp.exp(b), S)
    b_last = b[:, -1]
    S_new = jnp.exp(b_last)[..., None] * S + jnp.einsum('bshd,bshv->bhdv', k * jnp.exp(b_last[:, None] - b), v)
    return S_new, o_intra + o_inter


def gla_output(o, r, gnorm_g, w_o, dtype):
    B, T = o.shape[:2]
    o = o * lax.rsqrt(jnp.mean(o * o, axis=-1, keepdims=True) + RMS_EPS) * gnorm_g.astype(F32)
    o = o * jax.nn.silu(r.astype(F32).reshape(B, T, GLA_HEADS, GLA_HEAD_V))
    return o.reshape(B, T, GLA_VALUE_DIM).astype(dtype) @ w_o


def gla_prompt(h, w_in, w_gate2, b_gate, gnorm_g, w_o):
    B, L, _ = h.shape
    q, k, v, lg, r = gla_project(h, w_in, w_gate2, b_gate)
    S0 = jnp.zeros((B, GLA_HEADS, GLA_HEAD_K, GLA_HEAD_V), F32)
    S_meta, o_meta = gla_chunk(S0, q[:, :N_META], k[:, :N_META], v[:, :N_META], lg[:, :N_META])
    n_chunk = (L - N_META) // GLA_CHUNK

    def chunks(a):
        return jnp.moveaxis(a[:, N_META:].reshape(B, n_chunk, GLA_CHUNK, *a.shape[2:]), 1, 0)

    S_fin, o_ch = lax.scan(lambda S, xs: gla_chunk(S, *xs), S_meta, (chunks(q), chunks(k), chunks(v), chunks(lg)))
    o_real = jnp.moveaxis(o_ch, 0, 1).reshape(B, L - N_META, GLA_HEADS, GLA_HEAD_V)
    o = jnp.concatenate([o_meta, o_real], axis=1)
    return gla_output(o, r, gnorm_g, w_o, h.dtype), S_fin.astype(h.dtype)


def gla_sample(h, state, w_in, w_gate2, b_gate, gnorm_g, w_o):
    q, k, v, lg, r = gla_project(h, w_in, w_gate2, b_gate)
    S_new, o = gla_chunk(state.astype(F32), q, k, v, lg)
    return gla_output(o, r, gnorm_g, w_o, h.dtype), S_new.astype(state.dtype)


def swiglu(h, w_gu, w_down):
    g, u = jnp.split(h @ w_gu, 2, axis=-1)
    return (jax.nn.silu(g) * u) @ w_down


def moe_ffn(h, w_router, w_gu_e, w_down_e):
    logits = (h @ w_router).astype(F32)
    top_logit, top_idx = lax.top_k(logits, TOP_K)
    top_w = jax.nn.softmax(top_logit, axis=-1)
    combine = jnp.sum(jax.nn.one_hot(top_idx, N_EXPERTS, dtype=F32) * top_w[..., None], axis=-2).astype(h.dtype)
    out = jnp.zeros_like(h)
    for e in range(N_EXPERTS):
        out = out + combine[..., e:e + 1] * swiglu(h, w_gu_e[e], w_down_e[e])
    return out


def setup_inputs(seed: int = 0) -> dict:
    key = jax.random.key(seed)
    ks = iter(jax.random.split(key, 128))

    def nrm(shape, scale=1.0):
        return jax.random.normal(next(ks), shape, F32) * scale

    def lin(fan_in, fan_out, gain=1.0):
        return nrm((fan_in, fan_out), gain * fan_in ** -0.5)

    def gain_vec(n):
        return 1.0 + nrm((n,), 0.02)

    def bias_vec(n):
        return nrm((n,), 0.02)

    n_pages = PAST_LEN // PAGE_SIZE
    n_used = DEC_BATCH * n_pages
    n_pool = n_used + n_used // 4
    cache_shape = (n_pool, PAGE_SIZE, SB_HEADS, SB_HEAD_DIM)
    state_shape = (DEC_BATCH, GLA_HEADS, GLA_HEAD_K, GLA_HEAD_V)
    D = D_MODEL
    inp = {}
    inp['x_prompt'] = nrm((BATCH, SEQ, D))
    inp['x_sample'] = nrm((DEC_BATCH, DEC_SEQ, D))
    inp['cache_k0'] = nrm(cache_shape)
    inp['cache_v0'] = nrm(cache_shape, DEEPNORM_BETA)
    inp['state_s1'] = nrm(state_shape, GLA_HEAD_K ** -0.5)
    inp['cache_k2'] = nrm(cache_shape)
    inp['cache_v2'] = nrm(cache_shape, DEEPNORM_BETA)
    inp['state_s3'] = nrm(state_shape, GLA_HEAD_K ** -0.5)
    inp['page_table'] = jax.random.permutation(next(ks), n_pool)[:n_used].reshape(DEC_BATCH, n_pages).astype(jnp.int32)
    inp['meta_tokens'] = nrm((N_META, D))
    for i in range(DEPTH):
        p = 'l' + str(i) + '_'
        if i % 2 == 0:
            inp[p + 'w_qkv'] = jnp.concatenate([lin(D, D), lin(D, D), lin(D, D, DEEPNORM_BETA)], axis=1)
            inp[p + 'w_o'] = lin(D, D, DEEPNORM_BETA)
            inp[p + 'sb_bias'] = SB_LOGIT_BIAS_INIT + nrm((SB_HEADS,), 0.1)
            inp[p + 'ln1_g'] = gain_vec(D)
            inp[p + 'ln1_b'] = bias_vec(D)
            inp[p + 'w_gu'] = lin(D, 2 * D_FF, DEEPNORM_BETA)
            inp[p + 'w_down'] = lin(D_FF, D, DEEPNORM_BETA)
        else:
            inp[p + 'w_in'] = jnp.concatenate([lin(D, GLA_KEY_DIM), lin(D, GLA_KEY_DIM),
                                               lin(D, GLA_VALUE_DIM, DEEPNORM_BETA), lin(D, GLA_VALUE_DIM),
                                               lin(D, GLA_GATE_RANK)], axis=1)
            inp[p + 'w_gate2'] = lin(GLA_GATE_RANK, GLA_KEY_DIM)
            inp[p + 'b_gate'] = nrm((GLA_KEY_DIM,), 0.1)
            inp[p + 'gnorm_g'] = gain_vec(GLA_HEAD_V)
            inp[p + 'w_o'] = lin(GLA_VALUE_DIM, D, DEEPNORM_BETA)
            inp[p + 'ln1_g'] = gain_vec(D)
            inp[p + 'ln1_b'] = bias_vec(D)
            inp[p + 'w_router'] = lin(D, N_EXPERTS)
            inp[p + 'w_gu_e'] = nrm((N_EXPERTS, D, 2 * D_FF_EXPERT), DEEPNORM_BETA * D ** -0.5)
            inp[p + 'w_down_e'] = nrm((N_EXPERTS, D_FF_EXPERT, D), DEEPNORM_BETA * D_FF_EXPERT ** -0.5)
        inp[p + 'ln2_g'] = gain_vec(D)
        inp[p + 'ln2_b'] = bias_vec(D)
    return inp


def reference(x_prompt, x_sample, cache_k0, cache_v0, state_s1, cache_k2, cache_v2, state_s3, page_table, meta_tokens,
              l0_w_qkv, l0_w_o, l0_sb_bias, l0_ln1_g, l0_ln1_b, l0_w_gu, l0_w_down, l0_ln2_g, l0_ln2_b,
              l1_w_in, l1_w_gate2, l1_b_gate, l1_gnorm_g, l1_w_o, l1_ln1_g, l1_ln1_b,
              l1_w_router, l1_w_gu_e, l1_w_down_e, l1_ln2_g, l1_ln2_b,
              l2_w_qkv, l2_w_o, l2_sb_bias, l2_ln1_g, l2_ln1_b, l2_w_gu, l2_w_down, l2_ln2_g, l2_ln2_b,
              l3_w_in, l3_w_gate2, l3_b_gate, l3_gnorm_g, l3_w_o, l3_ln1_g, l3_ln1_b,
              l3_w_router, l3_w_gu_e, l3_w_down_e, l3_ln2_g, l3_ln2_b):
    B = x_prompt.shape[0]
    meta = jnp.broadcast_to(meta_tokens.astype(x_prompt.dtype)[None], (B, N_META, D_MODEL))
    hp = jnp.concatenate([meta, x_prompt], axis=1)
    hs = x_sample

    caches = [(cache_k0, cache_v0), (state_s1,), (cache_k2, cache_v2), (state_s3,)]
    mixers = [(l0_w_qkv, l0_w_o, l0_sb_bias),
              (l1_w_in, l1_w_gate2, l1_b_gate, l1_gnorm_g, l1_w_o),
              (l2_w_qkv, l2_w_o, l2_sb_bias),
              (l3_w_in, l3_w_gate2, l3_b_gate, l3_gnorm_g, l3_w_o)]
    norms1 = [(l0_ln1_g, l0_ln1_b), (l1_ln1_g, l1_ln1_b), (l2_ln1_g, l2_ln1_b), (l3_ln1_g, l3_ln1_b)]
    ffns = [(l0_w_gu, l0_w_down), (l1_w_router, l1_w_gu_e, l1_w_down_e),
            (l2_w_gu, l2_w_down), (l3_w_router, l3_w_gu_e, l3_w_down_e)]
    norms2 = [(l0_ln2_g, l0_ln2_b), (l1_ln2_g, l1_ln2_b), (l2_ln2_g, l2_ln2_b), (l3_ln2_g, l3_ln2_b)]

    new_state = []
    for i in range(DEPTH):
        if i % 2 == 0:
            mp, kp, vp = sb_prompt(hp, *mixers[i])
            ms, ks_, vs_ = sb_sample(hs, *caches[i], page_table, *mixers[i])
            new_state += [kp, vp, ks_, vs_]
        else:
            mp, sp = gla_prompt(hp, *mixers[i])
            ms, ss = gla_sample(hs, *caches[i], *mixers[i])
            new_state += [sp, ss]
        hp = deepnorm(hp, mp, *norms1[i])
        hs = deepnorm(hs, ms, *norms1[i])
        if i % 2 == 0:
            fp = swiglu(hp, *ffns[i])
            fs = swiglu(hs, *ffns[i])
        else:
            fp = moe_ffn(hp, *ffns[i])
            fs = moe_ffn(hs, *ffns[i])
        hp = deepnorm(hp, fp, *norms2[i])
        hs = deepnorm(hs, fs, *norms2[i])

    y_prompt = hp[:, N_META:]
    y_sample = hs
    (k0_prompt, v0_prompt, k0_sample, v0_sample, s1_prompt, s1_sample,
     k2_prompt, v2_prompt, k2_sample, v2_sample, s3_prompt, s3_sample) = new_state
    return (y_prompt, y_sample, k0_prompt, v0_prompt, k0_sample, v0_sample, s1_prompt, s1_sample,
            k2_prompt, v2_prompt, k2_sample, v2_sample, s3_prompt, s3_sample)
```

```python
import functools

import jax
import jax.numpy as jnp
from jax import lax
from jax.experimental import pallas as pl
from jax.experimental.pallas import tpu as pltpu

F32 = jnp.float32
BF16 = jnp.bfloat16

D_MODEL = 1024
N_META = 16
SB_HEADS = 16
SB_HEAD_DIM = 64
GLA_HEADS = 4
GLA_HEAD_K = 128
GLA_HEAD_V = 256
GLA_KEY_DIM = GLA_HEADS * GLA_HEAD_K
GLA_GATE_RANK = 16
GLA_TAU = 16.0
N_EXPERTS = 8
DEPTH = 4
DEEPNORM_ALPHA = float((2 * DEPTH) ** 0.25)
LN_EPS = 1e-5
RMS_EPS = 1e-6

LANES = 128
SUB_BF16 = 16
VMEM_LIMIT = 56 * 1024 * 1024

SB_TQ = 256
GLA_SUB = 16
GLA_CHUNK = 64
MOE_TM = 1024
MOE_TF = 512
ROUTE_TM = 1024
FFN_TF = 256


def _params(sem):
    return pltpu.CompilerParams(dimension_semantics=sem, vmem_limit_bytes=VMEM_LIMIT)


def _pick(m, cands):
    for c in cands:
        if m % c == 0:
            return c
    raise ValueError(f"no tile for {m} in {cands}")


def _dot(a, b):
    return jnp.dot(a, b, preferred_element_type=F32)


def _dot_nt(a, b):
    return lax.dot_general(a, b, (((1,), (1,)), ((), ())), preferred_element_type=F32)


def _dot_tn(a, b):
    return lax.dot_general(a, b, (((0,), (0,)), ((), ())), preferred_element_type=F32)


def _softplus(z):
    return jnp.maximum(z, 0.0) + jnp.log1p(jnp.exp(-jnp.abs(z)))


def _sigmoid(x):
    return 1.0 / (1.0 + jnp.exp(-x))


def _split2(x):
    hi = x.astype(BF16)
    lo = (x - hi.astype(F32)).astype(BF16)
    return hi, lo


def _split3(x):
    a = x.astype(BF16)
    r = x - a.astype(F32)
    b = r.astype(BF16)
    c = (r - b.astype(F32)).astype(BF16)
    return a, b, c


def _layer_norm(y, g, b):
    mu = jnp.mean(y, axis=-1, keepdims=True)
    yc = y - mu
    var = jnp.mean(yc * yc, axis=-1, keepdims=True)
    return yc * lax.rsqrt(var + LN_EPS) * g + b


def _linear_kernel(x_ref, w_ref, o_ref, wbf_ref):
    @pl.when(pl.program_id(1) == 0)
    def _():
        wbf_ref[...] = w_ref[...].astype(BF16)

    o_ref[...] = _dot(x_ref[...].astype(BF16), wbf_ref[...]).astype(o_ref.dtype)


def linear(x, w, col0, ncols, out_dtype, name):
    m, k = x.shape
    tm = _pick(m, (2064, 1024, 512, 256, 128, 16))
    tn = _pick(ncols, (512, 256, 128))
    assert col0 % tn == 0
    return pl.pallas_call(
        _linear_kernel,
        grid=(ncols // tn, m // tm),
        in_specs=[pl.BlockSpec((tm, k), lambda j, i: (i, 0)),
                  pl.BlockSpec((k, tn), lambda j, i: (0, col0 // tn + j))],
        out_specs=pl.BlockSpec((tm, tn), lambda j, i: (i, j)),
        out_shape=jax.ShapeDtypeStruct((m, ncols), out_dtype),
        scratch_shapes=[pltpu.VMEM((k, tn), BF16)],
        compiler_params=_params(("arbitrary", "arbitrary")),
        name=name,
    )(x, w)


def _linear_ln_kernel(x_ref, w_ref, res_ref, g_ref, b_ref, o32_ref, o16_ref, wbf_ref):
    @pl.when(pl.program_id(0) == 0)
    def _():
        wbf_ref[...] = w_ref[...].astype(BF16)

    y = _dot(x_ref[...].astype(BF16), wbf_ref[...])
    h = _layer_norm(DEEPNORM_ALPHA * res_ref[...] + y, g_ref[...], b_ref[...])
    o32_ref[...] = h
    o16_ref[...] = h.astype(BF16)


def linear_deepnorm(x, w, res, g, b, name):
    m, k = x.shape
    d = w.shape[1]
    tm = _pick(m, (688, 512, 256, 128, 16))
    row = lambda i: (i, 0)
    fix = lambda i: (0, 0)
    return pl.pallas_call(
        _linear_ln_kernel,
        grid=(m // tm,),
        in_specs=[pl.BlockSpec((tm, k), row), pl.BlockSpec((k, d), fix), pl.BlockSpec((tm, d), row),
                  pl.BlockSpec((1, d), fix), pl.BlockSpec((1, d), fix)],
        out_specs=[pl.BlockSpec((tm, d), row), pl.BlockSpec((tm, d), row)],
        out_shape=[jax.ShapeDtypeStruct((m, d), F32), jax.ShapeDtypeStruct((m, d), BF16)],
        scratch_shapes=[pltpu.VMEM((k, d), BF16)],
        compiler_params=_params(("arbitrary",)),
        name=name,
    )(x, w, res, g.reshape(1, d), b.reshape(1, d))


def _swiglu_ln_kernel(x_ref, wg_ref, wu_ref, wd_ref, res_ref, g_ref, b_ref, o32_ref, o16_ref, acc_ref):
    f = pl.program_id(1)

    @pl.when(f == 0)
    def _():
        acc_ref[...] = jnp.zeros_like(acc_ref)

    x = x_ref[...]
    gate = _dot(x, wg_ref[...].astype(BF16))
    up = _dot(x, wu_ref[...].astype(BF16))
    act = (gate * _sigmoid(gate) * up).astype(BF16)
    acc_ref[...] += _dot(act, wd_ref[...].astype(BF16))

    @pl.when(f == pl.num_programs(1) - 1)
    def _():
        h = _layer_norm(DEEPNORM_ALPHA * res_ref[...] + acc_ref[...], g_ref[...], b_ref[...])
        o32_ref[...] = h
        o16_ref[...] = h.astype(BF16)


def swiglu_deepnorm(x16, w_gu, w_down, res, g, b, name):
    m, d = x16.shape
    dff = w_down.shape[0]
    tm = _pick(m, (688, 512, 256, 128, 16))
    tf = FFN_TF
    nf = dff // tf
    assert nf * tf == dff
    row = lambda i, f: (i, 0)
    fix = lambda i, f: (0, 0)
    return pl.pallas_call(
        _swiglu_ln_kernel,
        grid=(m // tm, nf),
        in_specs=[pl.BlockSpec((tm, d), row),
                  pl.BlockSpec((d, tf), lambda i, f: (0, f)),
                  pl.BlockSpec((d, tf), lambda i, f: (0, nf + f)),
                  pl.BlockSpec((tf, d), lambda i, f: (f, 0)),
                  pl.BlockSpec((tm, d), row), pl.BlockSpec((1, d), fix), pl.BlockSpec((1, d), fix)],
        out_specs=[pl.BlockSpec((tm, d), row), pl.BlockSpec((tm, d), row)],
        out_shape=[jax.ShapeDtypeStruct((m, d), F32), jax.ShapeDtypeStruct((m, d), BF16)],
        scratch_shapes=[pltpu.VMEM((tm, d), F32)],
        compiler_params=_params(("arbitrary", "arbitrary")),
        name=name,
    )(x16, w_gu, w_gu, w_down, res, g.reshape(1, d), b.reshape(1, d))


def _strict_upper(n):
    r = lax.broadcasted_iota(jnp.int32, (n, n), 0)
    c = lax.broadcasted_iota(jnp.int32, (n, n), 1)
    return jnp.where(r > c, 1.0, 0.0).astype(BF16)


def _sb_tile(qh, kb, vb, bias, c, acc, upper, mask):
    z = _dot_nt(qh, kb) + bias
    sp = _softplus(z)
    log_keep = -sp
    log_beta = z - sp
    if mask is not None:
        log_keep = jnp.where(mask, log_keep, 0.0)
    hi, lo = _split2(log_keep)
    stick = _dot(hi, upper) + _dot(lo, upper)
    w = jnp.exp(log_beta + stick + c)
    if mask is not None:
        w = jnp.where(mask, w, 0.0)
    acc = acc + _dot(w.astype(BF16), vb)
    c = c + jnp.sum(log_keep, axis=1, keepdims=True)
    return c, acc


def _sb_prompt_kernel(bias_ref, q_ref, k_ref, v_ref, o_ref, kbf, vbf, *, seq, tq, nqb):
    p = pl.program_id(1)
    qi = pl.program_id(2)
    pad = nqb * tq - seq
    hd = SB_HEAD_DIM

    @pl.when(qi == 0)
    def _():
        kbf[0:seq, :] = k_ref[0].astype(BF16)
        vbf[0:seq, :] = v_ref[0].astype(BF16)
        if pad:
            kbf[seq:seq + pad, :] = jnp.zeros((pad, 2 * hd), BF16)
            vbf[seq:seq + pad, :] = jnp.zeros((pad, 2 * hd), BF16)

    def run(rows):
        q = q_ref[0, 0:rows, :] * (hd ** -0.5)
        lane = lax.broadcasted_iota(jnp.int32, (rows, 2 * hd), 1)
        rr = lax.broadcasted_iota(jnp.int32, (rows, tq), 0)
        cc = lax.broadcasted_iota(jnp.int32, (rows, tq), 1)
        diag_mask = cc < rr
        upper = _strict_upper(tq)
        outs = []
        for hh in range(2):
            in_head = (lane < hd) if hh == 0 else (lane >= hd)
            qh = jnp.where(in_head, q, jnp.zeros_like(q))
            bias = bias_ref[2 * p + hh]
            off = pl.multiple_of(qi * tq, tq)
            c0 = jnp.zeros((rows, 1), F32)
            a0 = jnp.zeros((rows, 2 * hd), F32)
            carry = _sb_tile(qh, kbf[pl.ds(off, tq), :], vbf[pl.ds(off, tq), :], bias, c0, a0, upper, diag_mask)

            def body(jj, carry):
                o2 = pl.multiple_of((qi - jj) * tq, tq)
                return _sb_tile(qh, kbf[pl.ds(o2, tq), :], vbf[pl.ds(o2, tq), :], bias, carry[0], carry[1],
                                upper, None)

            carry = lax.fori_loop(1, qi + 1, body, carry)
            outs.append(carry[1])
        o_ref[0, 0:rows, :] = jnp.where(lane < hd, outs[0], outs[1]).astype(o_ref.dtype)

    tail = seq - (nqb - 1) * tq
    if tail == tq:
        run(tq)
    else:
        pl.when(qi < nqb - 1)(lambda: run(tq))
        pl.when(qi == nqb - 1)(lambda: run(tail))


def sb_prompt_attention(q16, k32, v32, sb_bias, name):
    bsz, seq, d = q16.shape
    tq = SB_TQ
    nqb = pl.cdiv(seq, tq)
    assert (seq - (nqb - 1) * tq) % SUB_BF16 == 0
    w2 = 2 * SB_HEAD_DIM
    kern = functools.partial(_sb_prompt_kernel, seq=seq, tq=tq, nqb=nqb)
    return pl.pallas_call(
        kern,
        grid_spec=pltpu.PrefetchScalarGridSpec(
            num_scalar_prefetch=1,
            grid=(bsz, d // w2, nqb),
            in_specs=[pl.BlockSpec((1, tq, w2), lambda b, p, i, s: (b, i, p)),
                      pl.BlockSpec((1, seq, w2), lambda b, p, i, s: (b, 0, p)),
                      pl.BlockSpec((1, seq, w2), lambda b, p, i, s: (b, 0, p))],
            out_specs=pl.BlockSpec((1, tq, w2), lambda b, p, i, s: (b, i, p)),
            scratch_shapes=[pltpu.VMEM((nqb * tq, w2), BF16), pltpu.VMEM((nqb * tq, w2), BF16)]),
        out_shape=jax.ShapeDtypeStruct((bsz, seq, d), BF16),
        compiler_params=_params(("arbitrary", "arbitrary", "arbitrary")),
        name=name,
    )(sb_bias, q16, k32, v32)


def _sb_sample_kernel(pt_ref, bias_ref, q_ref, kn_ref, vn_ref, *rest, pps, t_new, page):
    k_pages = rest[:pps]
    v_pages = rest[pps:2 * pps]
    o_ref = rest[2 * pps]
    qbd_ref, bias_mat, c_ref, acc_ref = rest[2 * pps + 1:]
    s = pl.program_id(1)
    hd = SB_HEAD_DIM
    d = SB_HEADS * hd
    rows = SB_HEADS * t_new
    upper = _strict_upper(page)

    def tile(kb, vb, mask):
        c, acc = _sb_tile(qbd_ref[...], kb, vb, bias_mat[...], c_ref[...], acc_ref[...], upper, mask)
        c_ref[...] = c
        acc_ref[...] = acc

    @pl.when(s == 0)
    def _():
        q = q_ref[0] * (hd ** -0.5)
        qrep = jnp.concatenate([q] * SB_HEADS, axis=0)
        rh = lax.broadcasted_iota(jnp.int32, (rows, d), 0) // t_new
        ch = lax.broadcasted_iota(jnp.int32, (rows, d), 1) // hd
        qbd_ref[...] = jnp.where(rh == ch, qrep, 0.0).astype(BF16)
        rh2 = lax.broadcasted_iota(jnp.int32, (rows, page), 0) // t_new
        bm = jnp.zeros((rows, page), F32)
        for h in range(SB_HEADS):
            bm = jnp.where(rh2 == h, bias_ref[h], bm)
        bias_mat[...] = bm
        c_ref[...] = jnp.zeros_like(c_ref)
        acc_ref[...] = jnp.zeros_like(acc_ref)
        zpad = jnp.zeros((page - t_new, d), F32)
        kn = jnp.concatenate([kn_ref[0], zpad], axis=0).astype(BF16)
        vn = jnp.concatenate([vn_ref[0], zpad], axis=0).astype(BF16)
        tq = lax.broadcasted_iota(jnp.int32, (rows, page), 0) % t_new
        sk = lax.broadcasted_iota(jnp.int32, (rows, page), 1)
        tile(kn, vn, sk < tq)

    for r in range(pps):
        tile(k_pages[r][0].astype(BF16), v_pages[r][0].astype(BF16), None)

    @pl.when(s == pl.num_programs(1) - 1)
    def _():
        acc = acc_ref[...]
        ch = lax.broadcasted_iota(jnp.int32, (t_new, d), 1) // hd
        out = jnp.zeros((t_new, d), F32)
        for h in range(SB_HEADS):
            out = out + jnp.where(ch == h, acc[h * t_new:(h + 1) * t_new, :], 0.0)
        o_ref[0] = out


def sb_sample_attention(q, k_new, v_new, cache_k, cache_v, page_table, sb_bias, name):
    bsz, t_new, d = q.shape
    page = cache_k.shape[1]
    n_pages = page_table.shape[1]
    assert page == LANES and SB_HEADS * t_new == LANES
    pps = _pick(n_pages, (8, 4, 2, 1))
    nstep = n_pages // pps
    pt = page_table.reshape(-1).astype(jnp.int32)

    def page_map(r):
        return lambda b, s, pt_ref, bias_ref: (pt_ref[b * n_pages + (n_pages - 1 - (s * pps + r))], 0, 0)

    tok = pl.BlockSpec((1, t_new, d), lambda b, s, *_: (b, 0, 0))
    page_specs = [pl.BlockSpec((1, page, d), page_map(r)) for r in range(pps)]
    kern = functools.partial(_sb_sample_kernel, pps=pps, t_new=t_new, page=page)
    return pl.pallas_call(
        kern,
        grid_spec=pltpu.PrefetchScalarGridSpec(
            num_scalar_prefetch=2,
            grid=(bsz, nstep),
            in_specs=[tok, tok, tok] + page_specs + page_specs,
            out_specs=tok,
            scratch_shapes=[pltpu.VMEM((LANES, d), BF16), pltpu.VMEM((LANES, page), F32),
                            pltpu.VMEM((LANES, page), F32), pltpu.VMEM((LANES, d), F32)]),
        out_shape=jax.ShapeDtypeStruct((bsz, t_new, d), F32),
        compiler_params=_params(("arbitrary", "arbitrary")),
        name=name,
    )(pt, sb_bias, q, k_new, v_new, *([cache_k] * pps), *([cache_v] * pps))


def _gate_kernel(x_ref, wl_ref, w2_ref, bg_ref, o_ref):
    g_low = _dot(x_ref[...].astype(BF16), wl_ref[...].astype(BF16))
    a_hi, a_lo = _split2(g_low)
    w_hi, w_lo = _split2(w2_ref[...])
    pre = _dot(a_hi, w_hi) + _dot(a_hi, w_lo) + _dot(a_lo, w_hi) + bg_ref[...]
    o_ref[...] = -_softplus(-pre) * (1.0 / GLA_TAU)


def gla_log_gate(x, w_low, w_gate2, b_gate, name):
    m, k = x.shape
    rank, kd = w_gate2.shape
    wl = jnp.pad(w_low, ((0, 0), (0, LANES - rank)))
    w2 = jnp.pad(w_gate2, ((0, LANES - rank), (0, 0)))
    tm = _pick(m, (2064, 1024, 512, 256, 128, 16))
    row = lambda i: (i, 0)
    fix = lambda i: (0, 0)
    return pl.pallas_call(
        _gate_kernel,
        grid=(m // tm,),
        in_specs=[pl.BlockSpec((tm, k), row), pl.BlockSpec((k, LANES), fix), pl.BlockSpec((LANES, kd), fix),
                  pl.BlockSpec((1, kd), fix)],
        out_specs=pl.BlockSpec((tm, kd), row),
        out_shape=jax.ShapeDtypeStruct((m, kd), F32),
        compiler_params=_params(("arbitrary",)),
        name=name,
    )(x, wl, w2, b_gate.reshape(1, kd))


def _gla_chunk(q, k, v, lg, state):
    c, dk = q.shape
    dv = v.shape[1]
    sub = GLA_SUB
    nsub = c // sub
    ri = lax.broadcasted_iota(jnp.int32, (c, c), 0)
    ci = lax.broadcasted_iota(jnp.int32, (c, c), 1)
    lower = jnp.where(ri >= ci, 1.0, 0.0).astype(BF16)
    ones_c = jnp.ones((c, dk), BF16)
    parts = _split3(lg)
    b = _dot(lower, parts[0]) + _dot(lower, parts[1]) + _dot(lower, parts[2])
    b_end_col = _dot_tn(parts[0], ones_c) + _dot_tn(parts[1], ones_c) + _dot_tn(parts[2], ones_c)
    b_end = b[c - 1:c, :]

    o = _dot((q * jnp.exp(b)).astype(BF16), state.astype(BF16))
    k_end = (k * jnp.exp(b_end - b)).astype(BF16)
    decay = jnp.exp(b_end_col)
    new_state = jnp.concatenate([decay] * (dv // dk), axis=1) * state + _dot_tn(k_end, v)

    row16 = lax.broadcasted_iota(jnp.int32, (sub, dk), 0)
    lane16 = lax.broadcasted_iota(jnp.int32, (sub, LANES), 1)
    rowc = lax.broadcasted_iota(jnp.int32, (c, dk), 0)
    pieces = []
    for i in range(nsub):
        qs = q[i * sub:(i + 1) * sub]
        ks = k[i * sub:(i + 1) * sub]
        bs = b[i * sub:(i + 1) * sub]
        for s in range(sub):
            e = jnp.exp(jnp.where(row16 >= s, bs - bs[s:s + 1, :], -1e30))
            pieces.append((qs * e * ks[s:s + 1, :]).astype(BF16))
    pair = _dot(jnp.concatenate(pieces, axis=0), jnp.ones((dk, LANES), BF16))
    zrows = jnp.zeros((LANES - c, dk), BF16) if c < LANES else None
    score_rows = []
    for i in range(nsub):
        sc = jnp.zeros((sub, LANES), F32)
        for s in range(sub):
            blk = pair[(i * sub + s) * sub:(i * sub + s + 1) * sub]
            sc = jnp.where(lane16 == i * sub + s, blk, sc)
        if i > 0:
            b_ref = b[i * sub - 1:i * sub, :]
            qq = (q[i * sub:(i + 1) * sub] * jnp.exp(b[i * sub:(i + 1) * sub] - b_ref)).astype(BF16)
            kk = jnp.where(rowc < i * sub, k * jnp.exp(jnp.minimum(b_ref - b, 0.0)), 0.0).astype(BF16)
            if zrows is not None:
                kk = jnp.concatenate([kk, zrows], axis=0)
            sc = sc + _dot_nt(qq, kk)
        score_rows.append(sc)
    scores = jnp.concatenate(score_rows, axis=0).astype(BF16)
    vpad = v if c == LANES else jnp.concatenate([v, jnp.zeros((LANES - c, dv), BF16)], axis=0)
    o = o + _dot(scores, vpad)
    return o, new_state


def _gla_finish(o, r, gn):
    ms = jnp.mean(o * o, axis=-1, keepdims=True)
    return o * lax.rsqrt(ms + RMS_EPS) * gn * (r * _sigmoid(r))


def _gla_prompt_kernel(q_ref, k_ref, lg_ref, v_ref, r_ref, gn_ref, og_ref, s_ref, state, *, n0, chunk, nchunk):
    scale = GLA_HEAD_K ** -0.5
    gn = gn_ref[...]

    def step(start, c):
        q = q_ref[0, pl.ds(start, c), :] * scale
        o, s_new = _gla_chunk(q, k_ref[0, pl.ds(start, c), :], v_ref[0, pl.ds(start, c), :],
                              lg_ref[0, pl.ds(start, c), :], state[...])
        state[...] = s_new
        og_ref[0, pl.ds(start, c), :] = _gla_finish(o, r_ref[0, pl.ds(start, c), :].astype(F32), gn).astype(og_ref.dtype)

    state[...] = jnp.zeros_like(state)
    if n0:
        step(0, n0)

    def body(i, carry):
        step(pl.multiple_of(n0 + i * chunk, SUB_BF16), chunk)
        return carry

    lax.fori_loop(0, nchunk, body, 0)
    s_ref[0, 0] = state[...]


def gla_prompt(q, k, lg, v16, r16, gnorm_g, name):
    bsz, seq, _ = q.shape
    dk, dv, nh = GLA_HEAD_K, GLA_HEAD_V, GLA_HEADS
    n0 = N_META
    chunk = GLA_CHUNK
    nchunk = (seq - n0) // chunk
    assert n0 + nchunk * chunk == seq and n0 % GLA_SUB == 0
    kspec = pl.BlockSpec((1, seq, dk), lambda b, h: (b, 0, h))
    vspec = pl.BlockSpec((1, seq, dv), lambda b, h: (b, 0, h))
    kern = functools.partial(_gla_prompt_kernel, n0=n0, chunk=chunk, nchunk=nchunk)
    return pl.pallas_call(
        kern,
        grid=(bsz, nh),
        in_specs=[kspec, kspec, kspec, vspec, vspec, pl.BlockSpec((1, dv), lambda b, h: (0, 0))],
        out_specs=[vspec, pl.BlockSpec((1, 1, dk, dv), lambda b, h: (b, h, 0, 0))],
        out_shape=[jax.ShapeDtypeStruct((bsz, seq, nh * dv), BF16),
                   jax.ShapeDtypeStruct((bsz, nh, dk, dv), F32)],
        scratch_shapes=[pltpu.VMEM((dk, dv), F32)],
        compiler_params=_params(("arbitrary", "arbitrary")),
        name=name,
    )(q, k, lg, v16, r16, gnorm_g.reshape(1, dv))


def _gla_sample_kernel(q_ref, k_ref, lg_ref, v_ref, r_ref, gn_ref, s0_ref, og_ref, s_ref, *, t_new):
    scale = GLA_HEAD_K ** -0.5
    dk, dv = GLA_HEAD_K, GLA_HEAD_V
    gn = gn_ref[...]
    padk = jnp.zeros((GLA_SUB - t_new, dk), F32)
    padv = jnp.zeros((GLA_SUB - t_new, dv), F32)
    for h in range(GLA_HEADS):
        ks = slice(h * dk, (h + 1) * dk)
        vs = slice(h * dv, (h + 1) * dv)
        q = jnp.concatenate([q_ref[0, :, ks] * scale, padk], axis=0)
        k = jnp.concatenate([k_ref[0, :, ks], padk], axis=0)
        lg = jnp.concatenate([lg_ref[0, :, ks], padk], axis=0)
        v = jnp.concatenate([v_ref[0, :, vs], padv], axis=0).astype(BF16)
        o, s_new = _gla_chunk(q, k, v, lg, s0_ref[0, h])
        s_ref[0, h] = s_new
        og_ref[0, :, vs] = _gla_finish(o[0:t_new], r_ref[0, :, vs], gn)


def gla_sample(q, k, lg, v, r, gnorm_g, state, name):
    bsz, t_new, _ = q.shape
    dk, dv, nh = GLA_HEAD_K, GLA_HEAD_V, GLA_HEADS
    assert t_new <= GLA_SUB
    kspec = pl.BlockSpec((1, t_new, nh * dk), lambda b: (b, 0, 0))
    vspec = pl.BlockSpec((1, t_new, nh * dv), lambda b: (b, 0, 0))
    sspec = pl.BlockSpec((1, nh, dk, dv), lambda b: (b, 0, 0, 0))
    kern = functools.partial(_gla_sample_kernel, t_new=t_new)
    return pl.pallas_call(
        kern,
        grid=(bsz,),
        in_specs=[kspec, kspec, kspec, vspec, vspec, pl.BlockSpec((1, dv), lambda b: (0, 0)), sspec],
        out_specs=[vspec, sspec],
        out_shape=[jax.ShapeDtypeStruct((bsz, t_new, nh * dv), F32),
                   jax.ShapeDtypeStruct((bsz, nh, dk, dv), F32)],
        compiler_params=_params(("arbitrary",)),
        name=name,
    )(q, k, lg, v, r, gnorm_g.reshape(1, dv), state)


def _router_kernel(h_ref, wr_ref, meta_ref, cnt_ref, carry, *, n_tok, tm):
    i = pl.program_id(0)

    @pl.when(i == 0)
    def _():
        carry[...] = jnp.zeros_like(carry)

    hrow = lax.broadcasted_iota(jnp.int32, h_ref.shape, 0)
    h_hi, h_lo = _split2(jnp.where(i * tm + hrow < n_tok, h_ref[...], 0.0))
    w_hi, w_lo = _split2(wr_ref[...])
    logits = _dot(h_hi, w_hi) + _dot(h_lo, w_hi) + _dot(h_hi, w_lo)
    lane = lax.broadcasted_iota(jnp.int32, (tm, LANES), 1)
    row = lax.broadcasted_iota(jnp.int32, (tm, LANES), 0)
    neg = -jnp.inf
    l1 = jnp.where(lane < N_EXPERTS, logits, neg)
    m1 = jnp.max(l1, axis=1, keepdims=True)
    i1 = jnp.min(jnp.where(l1 == m1, lane, LANES), axis=1, keepdims=True)
    l2 = jnp.where(lane == i1, neg, l1)
    m2 = jnp.max(l2, axis=1, keepdims=True)
    i2 = jnp.min(jnp.where(l2 == m2, lane, LANES), axis=1, keepdims=True)
    e = jnp.exp(m2 - m1)
    w1 = 1.0 / (1.0 + e)
    w2 = e * w1
    valid = (i * tm + row) < n_tok
    hit = jnp.where(valid & ((lane == i1) | (lane == i2)), 1.0, 0.0)
    ri = lax.broadcasted_iota(jnp.int32, (tm, tm), 0)
    ci = lax.broadcasted_iota(jnp.int32, (tm, tm), 1)
    before = jnp.where(ri > ci, 1.0, 0.0).astype(BF16)
    rank = _dot(before, hit.astype(BF16)) + carry[0:1, :]
    r1 = jnp.sum(jnp.where(lane == i1, rank, 0.0), axis=1, keepdims=True)
    r2 = jnp.sum(jnp.where(lane == i2, rank, 0.0), axis=1, keepdims=True)
    carry[0:1, :] = carry[0:1, :] + jnp.sum(hit, axis=0, keepdims=True)
    meta = jnp.where(lane == 0, i1.astype(F32), 0.0)
    meta = jnp.where(lane == 1, i2.astype(F32), meta)
    meta = jnp.where(lane == 2, w1, meta)
    meta = jnp.where(lane == 3, w2, meta)
    meta = jnp.where(lane == 4, r1, meta)
    meta = jnp.where(lane == 5, r2, meta)
    meta_ref[...] = meta

    @pl.when(i == pl.num_programs(0) - 1)
    def _():
        cnt_ref[...] = carry[...]


def moe_route(h32, w_router):
    n_tok, d = h32.shape
    tm = ROUTE_TM
    nt = pl.cdiv(n_tok, tm)
    wr = jnp.pad(w_router, ((0, 0), (0, LANES - w_router.shape[1])))
    kern = functools.partial(_router_kernel, n_tok=n_tok, tm=tm)
    return pl.pallas_call(
        kern,
        grid=(nt,),
        in_specs=[pl.BlockSpec((tm, d), lambda i: (i, 0)), pl.BlockSpec((d, LANES), lambda i: (0, 0))],
        out_specs=[pl.BlockSpec((tm, LANES), lambda i: (i, 0)), pl.BlockSpec((8, LANES), lambda i: (0, 0))],
        out_shape=[jax.ShapeDtypeStruct((nt * tm, LANES), F32), jax.ShapeDtypeStruct((8, LANES), F32)],
        scratch_shapes=[pltpu.VMEM((8, LANES), F32)],
        compiler_params=_params(("arbitrary",)),
        name="moe_route",
    )(h32, wr)


def _row_copy(src, dst, i, j, sem):
    return pltpu.make_async_copy(src.at[pl.ds(i, 1)], dst.at[pl.ds(j, 1)], sem)


def _dispatch_kernel(pos_ref, h_hbm, xs_in, xs_out, sem, *, tm):
    del xs_in
    base = pl.program_id(0) * tm

    def for_each(fn):
        def body(r, carry):
            for slot in range(2):
                p = pos_ref[0, 0, 2 * r + slot]

                @pl.when(p >= 0)
                def _():
                    fn(_row_copy(h_hbm, xs_out, base + r, p, sem))
            return carry
        lax.fori_loop(0, tm, body, 0)

    for_each(lambda cp: cp.start())
    for_each(lambda cp: cp.wait())


def moe_dispatch(h32, pos, n_rows):
    n_tok, d = h32.shape
    tm = ROUTE_TM
    nt = pl.cdiv(n_tok, tm)
    pos = jnp.pad(pos, ((0, nt * tm - n_tok), (0, 0)), constant_values=-1).reshape(nt, 1, 2 * tm)
    kern = functools.partial(_dispatch_kernel, tm=tm)
    any_spec = pl.BlockSpec(memory_space=pl.ANY)
    return pl.pallas_call(
        kern,
        grid=(nt,),
        in_specs=[pl.BlockSpec((1, 1, 2 * tm), lambda i: (i, 0, 0), memory_space=pltpu.SMEM), any_spec, any_spec],
        out_specs=any_spec,
        out_shape=jax.ShapeDtypeStruct((n_rows, d), F32),
        scratch_shapes=[pltpu.SemaphoreType.DMA(())],
        input_output_aliases={2: 0},
        compiler_params=_params(("arbitrary",)),
        name="moe_dispatch",
    )(pos, h32, jnp.zeros((n_rows, d), F32))


def _experts_kernel(te_ref, tv_ref, x_ref, wg_ref, wu_ref, wd_ref, y_ref, xbf, acc):
    t = pl.program_id(0)
    f = pl.program_id(1)
    live = tv_ref[t] > 0

    @pl.when(f == 0)
    def _():
        xbf[...] = x_ref[...].astype(BF16)
        acc[...] = jnp.zeros_like(acc)

    @pl.when(live)
    def _():
        x = xbf[...]
        gate = _dot(x, wg_ref[0].astype(BF16))
        up = _dot(x, wu_ref[0].astype(BF16))
        act = (gate * _sigmoid(gate) * up).astype(BF16)
        acc[...] += _dot(act, wd_ref[0].astype(BF16))

    @pl.when(f == pl.num_programs(1) - 1)
    def _():
        y_ref[...] = acc[...]


def moe_experts(xs, w_gu_e, w_down_e, tile_expert, tile_live):
    n_rows, d = xs.shape
    dff = w_down_e.shape[1]
    tm, tf = MOE_TM, MOE_TF
    nf = dff // tf
    assert nf * tf == dff and n_rows % tm == 0

    def fcol(t, f, tv):
        return jnp.where(tv[t] > 0, f, nf - 1)

    return pl.pallas_call(
        _experts_kernel,
        grid_spec=pltpu.PrefetchScalarGridSpec(
            num_scalar_prefetch=2,
            grid=(n_rows // tm, nf),
            in_specs=[pl.BlockSpec((tm, d), lambda t, f, te, tv: (t, 0)),
                      pl.BlockSpec((1, d, tf), lambda t, f, te, tv: (te[t], 0, fcol(t, f, tv))),
                      pl.BlockSpec((1, d, tf), lambda t, f, te, tv: (te[t], 0, nf + fcol(t, f, tv))),
                      pl.BlockSpec((1, tf, d), lambda t, f, te, tv: (te[t], fcol(t, f, tv), 0))],
            out_specs=pl.BlockSpec((tm, d), lambda t, f, te, tv: (t, 0)),
            scratch_shapes=[pltpu.VMEM((tm, d), BF16), pltpu.VMEM((tm, d), F32)]),
        out_shape=jax.ShapeDtypeStruct((n_rows, d), F32),
        compiler_params=_params(("arbitrary", "arbitrary")),
        name="moe_experts",
    )(tile_expert, tile_live, xs, w_gu_e, w_gu_e, w_down_e)


def _combine_kernel(pos_ref, meta_ref, res_ref, g_ref, b_ref, y_hbm, o32_ref, o16_ref, y1, y2, sem, *, tm):
    def for_each(fn):
        def body(r, carry):
            fn(_row_copy(y_hbm, y1, pos_ref[0, 0, 2 * r], r, sem))
            fn(_row_copy(y_hbm, y2, pos_ref[0, 0, 2 * r + 1], r, sem))
            return carry
        lax.fori_loop(0, tm, body, 0)

    for_each(lambda cp: cp.start())
    for_each(lambda cp: cp.wait())
    meta = meta_ref[...]
    mix = meta[:, 2:3] * y1[...] + meta[:, 3:4] * y2[...]
    h = _layer_norm(DEEPNORM_ALPHA * res_ref[...] + mix, g_ref[...], b_ref[...])
    o32_ref[...] = h
    o16_ref[...] = h.astype(BF16)


def moe_combine_deepnorm(ys, pos, meta, res, g, b, name):
    m, d = res.shape
    tm = _pick(m, (688, 512, 256, 128, 16))
    nt = m // tm
    row = lambda i: (i, 0)
    fix = lambda i: (0, 0)
    kern = functools.partial(_combine_kernel, tm=tm)
    return pl.pallas_call(
        kern,
        grid=(nt,),
        in_specs=[pl.BlockSpec((1, 1, 2 * tm), lambda i: (i, 0, 0), memory_space=pltpu.SMEM),
                  pl.BlockSpec((tm, LANES), row), pl.BlockSpec((tm, d), row),
                  pl.BlockSpec((1, d), fix), pl.BlockSpec((1, d), fix), pl.BlockSpec(memory_space=pl.ANY)],
        out_specs=[pl.BlockSpec((tm, d), row), pl.BlockSpec((tm, d), row)],
        out_shape=[jax.ShapeDtypeStruct((m, d), F32), jax.ShapeDtypeStruct((m, d), BF16)],
        scratch_shapes=[pltpu.VMEM((tm, d), F32), pltpu.VMEM((tm, d), F32), pltpu.SemaphoreType.DMA(())],
        compiler_params=_params(("arbitrary",)),
        name=name,
    )(pos.reshape(nt, 1, 2 * tm), meta, res, g.reshape(1, d), b.reshape(1, d), ys)


def moe_deepnorm(hp32, hs32, w_router, w_gu_e, w_down_e, g, b, layer):
    mp = hp32.shape[0]
    h_all = jnp.concatenate([hp32, hs32], axis=0)
    n_tok = h_all.shape[0]
    meta, cnt = moe_route(h_all, w_router)
    meta = meta[:n_tok]
    counts = cnt[0, :N_EXPERTS].astype(jnp.int32)
    padded = ((counts + MOE_TM - 1) // MOE_TM) * MOE_TM
    ends = jnp.cumsum(padded)
    starts = ends - padded
    e1 = meta[:, 0].astype(jnp.int32)
    e2 = meta[:, 1].astype(jnp.int32)
    eids = jnp.arange(N_EXPERTS, dtype=jnp.int32)
    start1 = jnp.sum(jnp.where(e1[:, None] == eids[None, :], starts[None, :], 0), axis=1)
    start2 = jnp.sum(jnp.where(e2[:, None] == eids[None, :], starts[None, :], 0), axis=1)
    pos = jnp.stack([start1 + meta[:, 4].astype(jnp.int32), start2 + meta[:, 5].astype(jnp.int32)], axis=1)
    n_tiles = pl.cdiv(2 * n_tok, MOE_TM) + N_EXPERTS
    tile_start = jnp.arange(n_tiles, dtype=jnp.int32) * MOE_TM
    tile_live = (tile_start < ends[-1]).astype(jnp.int32)
    tile_expert = jnp.sum((tile_start[:, None] >= ends[None, :]).astype(jnp.int32), axis=1)
    last_live = jnp.sum((jnp.maximum(ends[-1] - 1, 0) >= ends).astype(jnp.int32))
    tile_expert = jnp.where(tile_live > 0, tile_expert, last_live).astype(jnp.int32)

    xs = moe_dispatch(h_all, pos, n_tiles * MOE_TM)
    ys = moe_experts(xs, w_gu_e, w_down_e, tile_expert, tile_live)
    outp = moe_combine_deepnorm(ys, pos[:mp], meta[:mp], hp32, g, b, f"l{layer}_moe_combine_p")
    outs = moe_combine_deepnorm(ys, pos[mp:], meta[mp:], hs32, g, b, f"l{layer}_moe_combine_s")
    return outp, outs


def _sb_layer(i, hp32, hp16, hs32, hs16, bsz, seq, dec_b, dec_t, cache_k, cache_v, page_table,
              w_qkv, w_o, sb_bias, ln1_g, ln1_b):
    d = D_MODEL
    tag = f"l{i}_"
    qp = linear(hp16, w_qkv, 0, d, BF16, tag + "q_p")
    kp = linear(hp16, w_qkv, d, d, F32, tag + "k_p")
    vp = linear(hp16, w_qkv, 2 * d, d, F32, tag + "v_p")
    qs = linear(hs16, w_qkv, 0, d, F32, tag + "q_s")
    ks = linear(hs16, w_qkv, d, d, F32, tag + "k_s")
    vs = linear(hs16, w_qkv, 2 * d, d, F32, tag + "v_s")
    op = sb_prompt_attention(qp.reshape(bsz, seq, d), kp.reshape(bsz, seq, d), vp.reshape(bsz, seq, d),
                             sb_bias, tag + "sb_p")
    pool, page = cache_k.shape[0], cache_k.shape[1]
    os_ = sb_sample_attention(qs.reshape(dec_b, dec_t, d), ks.reshape(dec_b, dec_t, d), vs.reshape(dec_b, dec_t, d),
                              cache_k.reshape(pool, page, d), cache_v.reshape(pool, page, d), page_table,
                              sb_bias, tag + "sb_s")
    hp32, hp16 = linear_deepnorm(op.reshape(bsz * seq, d), w_o, hp32, ln1_g, ln1_b, tag + "o_p")
    hs32, hs16 = linear_deepnorm(os_.reshape(dec_b * dec_t, d), w_o, hs32, ln1_g, ln1_b, tag + "o_s")
    shp = (bsz, seq, SB_HEADS, SB_HEAD_DIM)
    shs = (dec_b, dec_t, SB_HEADS, SB_HEAD_DIM)
    new_state = [kp.reshape(shp), vp.reshape(shp), ks.reshape(shs), vs.reshape(shs)]
    return hp32, hp16, hs32, hs16, new_state


def _gla_layer(i, hp32, hp16, hs32, hs16, bsz, seq, dec_b, dec_t, state,
               w_in, w_gate2, b_gate, gnorm_g, w_o, ln1_g, ln1_b):
    tag = f"l{i}_"
    kd, vd = GLA_KEY_DIM, GLA_HEADS * GLA_HEAD_V
    w_low = w_in[:, 2 * kd + 2 * vd:]

    def project(x16, vdtype, sfx):
        q = linear(x16, w_in, 0, kd, F32, tag + "q" + sfx)
        k = linear(x16, w_in, kd, kd, F32, tag + "k" + sfx)
        v = linear(x16, w_in, 2 * kd, vd, vdtype, tag + "v" + sfx)
        r = linear(x16, w_in, 2 * kd + vd, vd, vdtype, tag + "r" + sfx)
        lg = gla_log_gate(x16, w_low, w_gate2, b_gate, tag + "gate" + sfx)
        return q, k, lg, v, r

    q, k, lg, v, r = project(hp16, BF16, "_p")
    og_p, sp = gla_prompt(q.reshape(bsz, seq, kd), k.reshape(bsz, seq, kd), lg.reshape(bsz, seq, kd),
                          v.reshape(bsz, seq, vd), r.reshape(bsz, seq, vd), gnorm_g, tag + "gla_p")
    q, k, lg, v, r = project(hs16, F32, "_s")
    og_s, ss = gla_sample(q.reshape(dec_b, dec_t, kd), k.reshape(dec_b, dec_t, kd), lg.reshape(dec_b, dec_t, kd),
                          v.reshape(dec_b, dec_t, vd), r.reshape(dec_b, dec_t, vd), gnorm_g, state, tag + "gla_s")
    hp32, hp16 = linear_deepnorm(og_p.reshape(bsz * seq, vd), w_o, hp32, ln1_g, ln1_b, tag + "o_p")
    hs32, hs16 = linear_deepnorm(og_s.reshape(dec_b * dec_t, vd), w_o, hs32, ln1_g, ln1_b, tag + "o_s")
    return hp32, hp16, hs32, hs16, [sp, ss]


def kernel(x_prompt, x_sample, cache_k0, cache_v0, state_s1, cache_k2, cache_v2, state_s3, page_table, meta_tokens, l0_w_qkv, l0_w_o, l0_sb_bias, l0_ln1_g, l0_ln1_b, l0_w_gu, l0_w_down, l0_ln2_g, l0_ln2_b, l1_w_in, l1_w_gate2, l1_b_gate, l1_gnorm_g, l1_w_o, l1_ln1_g, l1_ln1_b, l1_w_router, l1_w_gu_e, l1_w_down_e, l1_ln2_g, l1_ln2_b, l2_w_qkv, l2_w_o, l2_sb_bias, l2_ln1_g, l2_ln1_b, l2_w_gu, l2_w_down, l2_ln2_g, l2_ln2_b, l3_w_in, l3_w_gate2, l3_b_gate, l3_gnorm_g, l3_w_o, l3_ln1_g, l3_ln1_b, l3_w_router, l3_w_gu_e, l3_w_down_e, l3_ln2_g, l3_ln2_b):
    bsz, seq0, d = x_prompt.shape
    dec_b, dec_t, _ = x_sample.shape
    n_meta = meta_tokens.shape[0]
    seq = n_meta + seq0
    meta = jnp.broadcast_to(meta_tokens.astype(x_prompt.dtype)[None], (bsz, n_meta, d))
    hp32 = jnp.concatenate([meta, x_prompt], axis=1).reshape(bsz * seq, d)
    hs32 = x_sample.reshape(dec_b * dec_t, d)
    hp16 = hp32.astype(BF16)
    hs16 = hs32.astype(BF16)

    sb = [(cache_k0, cache_v0, l0_w_qkv, l0_w_o, l0_sb_bias, l0_ln1_g, l0_ln1_b),
          (cache_k2, cache_v2, l2_w_qkv, l2_w_o, l2_sb_bias, l2_ln1_g, l2_ln1_b)]
    ffn = [(l0_w_gu, l0_w_down, l0_ln2_g, l0_ln2_b), (l2_w_gu, l2_w_down, l2_ln2_g, l2_ln2_b)]
    gla = [(state_s1, l1_w_in, l1_w_gate2, l1_b_gate, l1_gnorm_g, l1_w_o, l1_ln1_g, l1_ln1_b),
           (state_s3, l3_w_in, l3_w_gate2, l3_b_gate, l3_gnorm_g, l3_w_o, l3_ln1_g, l3_ln1_b)]
    moe = [(l1_w_router, l1_w_gu_e, l1_w_down_e, l1_ln2_g, l1_ln2_b),
           (l3_w_router, l3_w_gu_e, l3_w_down_e, l3_ln2_g, l3_ln2_b)]

    new_state = []
    for i in range(DEPTH):
        if i % 2 == 0:
            cache_k, cache_v, w_qkv, w_o, sb_bias, g1, b1 = sb[i // 2]
            hp32, hp16, hs32, hs16, st = _sb_layer(i, hp32, hp16, hs32, hs16, bsz, seq, dec_b, dec_t,
                                                   cache_k, cache_v, page_table, w_qkv, w_o, sb_bias, g1, b1)
            w_gu, w_down, g2, b2 = ffn[i // 2]
            hp32, hp16 = swiglu_deepnorm(hp16, w_gu, w_down, hp32, g2, b2, f"l{i}_ffn_p")
            hs32, hs16 = swiglu_deepnorm(hs16, w_gu, w_down, hs32, g2, b2, f"l{i}_ffn_s")
        else:
            state, w_in, w_gate2, b_gate, gnorm_g, w_o, g1, b1 = gla[i // 2]
            hp32, hp16, hs32, hs16, st = _gla_layer(i, hp32, hp16, hs32, hs16, bsz, seq, dec_b, dec_t, state,
                                                    w_in, w_gate2, b_gate, gnorm_g, w_o, g1, b1)
            w_router, w_gu_e, w_down_e, g2, b2 = moe[i // 2]
            (hp32, hp16), (hs32, hs16) = moe_deepnorm(hp32, hs32, w_router, w_gu_e, w_down_e, g2, b2, i)
        new_state += st

    y_prompt = hp32.reshape(bsz, seq, d)[:, n_meta:]
    y_sample = hs32.reshape(dec_b, dec_t, d)
    return (y_prompt, y_sample, *new_state)
```

```python
import functools

import jax
import jax.numpy as jnp
from jax import lax
from jax.experimental import pallas as pl
from jax.experimental.pallas import tpu as pltpu

F32 = jnp.float32
BF16 = jnp.bfloat16

D_MODEL = 1024
N_META = 16
SB_HEADS = 16
SB_HEAD_DIM = 64
GLA_HEADS = 4
GLA_HEAD_K = 128
GLA_HEAD_V = 256
GLA_KEY_DIM = GLA_HEADS * GLA_HEAD_K
GLA_GATE_RANK = 16
GLA_TAU = 16.0
N_EXPERTS = 8
DEPTH = 4
DEEPNORM_ALPHA = float((2 * DEPTH) ** 0.25)
LN_EPS = 1e-5
RMS_EPS = 1e-6

LANES = 128
SUB_BF16 = 16
VMEM_LIMIT = 56 * 1024 * 1024

SB_TQ = 256
GLA_SUB = 16
GLA_CHUNK = 64
MOE_TM = 1024
MOE_TF = 512
ROUTE_TM = 1024
FFN_TF = 256


def _params(sem):
    return pltpu.CompilerParams(dimension_semantics=sem, vmem_limit_bytes=VMEM_LIMIT)


def _pick(m, cands):
    for c in cands:
        if m % c == 0:
            return c
    raise ValueError(f"no tile for {m} in {cands}")


def _dot(a, b):
    return jnp.dot(a, b, preferred_element_type=F32)


def _dot_nt(a, b):
    return lax.dot_general(a, b, (((1,), (1,)), ((), ())), preferred_element_type=F32)


def _dot_tn(a, b):
    return lax.dot_general(a, b, (((0,), (0,)), ((), ())), preferred_element_type=F32)


def _softplus(z):
    return jnp.maximum(z, 0.0) + jnp.log1p(jnp.exp(-jnp.abs(z)))


def _sigmoid(x):
    return 1.0 / (1.0 + jnp.exp(-x))


def _split2(x):
    hi = x.astype(BF16)
    lo = (x - hi.astype(F32)).astype(BF16)
    return hi, lo


def _split3(x):
    a = x.astype(BF16)
    r = x - a.astype(F32)
    b = r.astype(BF16)
    c = (r - b.astype(F32)).astype(BF16)
    return a, b, c


def _layer_norm(y, g, b):
    mu = jnp.mean(y, axis=-1, keepdims=True)
    yc = y - mu
    var = jnp.mean(yc * yc, axis=-1, keepdims=True)
    return yc * lax.rsqrt(var + LN_EPS) * g + b


def _linear_kernel(x_ref, w_ref, o_ref, wbf_ref):
    @pl.when(pl.program_id(1) == 0)
    def _():
        wbf_ref[...] = w_ref[...].astype(BF16)

    o_ref[...] = _dot(x_ref[...].astype(BF16), wbf_ref[...]).astype(o_ref.dtype)


def linear(x, w, col0, ncols, out_dtype, name):
    m, k = x.shape
    tm = _pick(m, (2064, 1024, 512, 256, 128, 16))
    tn = _pick(ncols, (512, 256, 128))
    assert col0 % tn == 0
    return pl.pallas_call(
        _linear_kernel,
        grid=(ncols // tn, m // tm),
        in_specs=[pl.BlockSpec((tm, k), lambda j, i: (i, 0)),
                  pl.BlockSpec((k, tn), lambda j, i: (0, col0 // tn + j))],
        out_specs=pl.BlockSpec((tm, tn), lambda j, i: (i, j)),
        out_shape=jax.ShapeDtypeStruct((m, ncols), out_dtype),
        scratch_shapes=[pltpu.VMEM((k, tn), BF16)],
        compiler_params=_params(("arbitrary", "arbitrary")),
        name=name,
    )(x, w)


def _linear_ln_kernel(x_ref, w_ref, res_ref, g_ref, b_ref, o32_ref, o16_ref, wbf_ref):
    @pl.when(pl.program_id(0) == 0)
    def _():
        wbf_ref[...] = w_ref[...].astype(BF16)

    y = _dot(x_ref[...].astype(BF16), wbf_ref[...])
    h = _layer_norm(DEEPNORM_ALPHA * res_ref[...] + y, g_ref[...], b_ref[...])
    o32_ref[...] = h
    o16_ref[...] = h.astype(BF16)


def linear_deepnorm(x, w, res, g, b, name):
    m, k = x.shape
    d = w.shape[1]
    tm = _pick(m, (688, 512, 256, 128, 16))
    row = lambda i: (i, 0)
    fix = lambda i: (0, 0)
    return pl.pallas_call(
        _linear_ln_kernel,
        grid=(m // tm,),
        in_specs=[pl.BlockSpec((tm, k), row), pl.BlockSpec((k, d), fix), pl.BlockSpec((tm, d), row),
                  pl.BlockSpec((1, d), fix), pl.BlockSpec((1, d), fix)],
        out_specs=[pl.BlockSpec((tm, d), row), pl.BlockSpec((tm, d), row)],
        out_shape=[jax.ShapeDtypeStruct((m, d), F32), jax.ShapeDtypeStruct((m, d), BF16)],
        scratch_shapes=[pltpu.VMEM((k, d), BF16)],
        compiler_params=_params(("arbitrary",)),
        name=name,
    )(x, w, res, g.reshape(1, d), b.reshape(1, d))


def _swiglu_ln_kernel(x_ref, wg_ref, wu_ref, wd_ref, res_ref, g_ref, b_ref, o32_ref, o16_ref, acc_ref):
    f = pl.program_id(1)

    @pl.when(f == 0)
    def _():
        acc_ref[...] = jnp.zeros_like(acc_ref)

    x = x_ref[...]
    gate = _dot(x, wg_ref[...].astype(BF16))
    up = _dot(x, wu_ref[...].astype(BF16))
    act = (gate * _sigmoid(gate) * up).astype(BF16)
    acc_ref[...] += _dot(act, wd_ref[...].astype(BF16))

    @pl.when(f == pl.num_programs(1) - 1)
    def _():
        h = _layer_norm(DEEPNORM_ALPHA * res_ref[...] + acc_ref[...], g_ref[...], b_ref[...])
        o32_ref[...] = h
        o16_ref[...] = h.astype(BF16)


def swiglu_deepnorm(x16, w_gu, w_down, res, g, b, name):
    m, d = x16.shape
    dff = w_down.shape[0]
    tm = _pick(m, (688, 512, 256, 128, 16))
    tf = FFN_TF
    nf = dff // tf
    assert nf * tf == dff
    row = lambda i, f: (i, 0)
    fix = lambda i, f: (0, 0)
    return pl.pallas_call(
        _swiglu_ln_kernel,
        grid=(m // tm, nf),
        in_specs=[pl.BlockSpec((tm, d), row),
                  pl.BlockSpec((d, tf), lambda i, f: (0, f)),
                  pl.BlockSpec((d, tf), lambda i, f: (0, nf + f)),
                  pl.BlockSpec((tf, d), lambda i, f: (f, 0)),
                  pl.BlockSpec((tm, d), row), pl.BlockSpec((1, d), fix), pl.BlockSpec((1, d), fix)],
        out_specs=[pl.BlockSpec((tm, d), row), pl.BlockSpec((tm, d), row)],
        out_shape=[jax.ShapeDtypeStruct((m, d), F32), jax.ShapeDtypeStruct((m, d), BF16)],
        scratch_shapes=[pltpu.VMEM((tm, d), F32)],
        compiler_params=_params(("arbitrary", "arbitrary")),
        name=name,
    )(x16, w_gu, w_gu, w_down, res, g.reshape(1, d), b.reshape(1, d))


def _neg_strict_upper(n):
    r = lax.broadcasted_iota(jnp.int32, (n, n), 0)
    c = lax.broadcasted_iota(jnp.int32, (n, n), 1)
    return jnp.where(r > c, -1.0, 0.0).astype(BF16)


def _sb_tiles(qs, biases, blocks, carry, neg_upper, keys_on_lanes=False):
    tiles = [(h, b) for b in range(len(blocks)) for h in range(len(qs))]
    z = {}
    for h, b in tiles:
        kb = blocks[b][0]
        z[h, b] = (_dot(qs[h], kb) if keys_on_lanes else _dot_nt(qs[h], kb)) + biases[h]
    hi, lo, log_beta, c_at = {}, {}, {}, {}
    c_run = [cr[0] for cr in carry]
    for h, b in tiles:
        mask = blocks[b][2]
        sp = jnp.maximum(z[h, b], 0.0) + jnp.log(1.0 + jnp.exp(-jnp.abs(z[h, b])))
        log_beta[h, b] = z[h, b] - sp
        if mask is not None:
            sp = jnp.where(mask, sp, 0.0)
        hi[h, b], lo[h, b] = _split2(sp)
        c_at[h, b] = c_run[h]
        c_run[h] = c_run[h] - jnp.sum(sp, axis=1, keepdims=True)
    stick = {}
    upper2 = jnp.concatenate([neg_upper, neg_upper], axis=0)
    for h, b in tiles:
        stick[h, b] = _dot(jnp.concatenate([hi[h, b], lo[h, b]], axis=1), upper2)
    w = {}
    for h, b in tiles:
        mask = blocks[b][2]
        wt = jnp.exp(log_beta[h, b] + stick[h, b] + c_at[h, b])
        if mask is not None:
            wt = jnp.where(mask, wt, 0.0)
        w[h, b] = wt.astype(BF16)
    acc = [cr[1] for cr in carry]
    for h, b in tiles:
        vb = blocks[b][1]
        acc[h] = acc[h] + (_dot_nt(w[h, b], vb) if keys_on_lanes else _dot(w[h, b], vb))
    return tuple((c_run[h], acc[h]) for h in range(len(qs)))


def _sb_prompt_kernel(bias_ref, q_ref, k_ref, v_ref, o_ref, kbf, vbf, *, seq, tq, nqb):
    p = pl.program_id(1)
    qi = pl.program_id(2)
    pad = nqb * tq - seq
    hd = SB_HEAD_DIM

    @pl.when(qi == 0)
    def _():
        kbf[0:seq, :] = k_ref[0].astype(BF16)
        vbf[0:seq, :] = v_ref[0].astype(BF16)
        if pad:
            kbf[seq:seq + pad, :] = jnp.zeros((pad, 2 * hd), BF16)
            vbf[seq:seq + pad, :] = jnp.zeros((pad, 2 * hd), BF16)

    def run(rows):
        q = q_ref[0, 0:rows, :] * (hd ** -0.5)
        lane = lax.broadcasted_iota(jnp.int32, (rows, 2 * hd), 1)
        rr = lax.broadcasted_iota(jnp.int32, (rows, tq), 0)
        cc = lax.broadcasted_iota(jnp.int32, (rows, tq), 1)
        diag_mask = cc < rr
        neg_upper = _neg_strict_upper(tq)
        zero = jnp.zeros_like(q)
        q_heads = (jnp.where(lane < hd, q, zero), jnp.where(lane >= hd, q, zero))
        biases = (bias_ref[2 * p], bias_ref[2 * p + 1])

        def block(j, mask):
            off = pl.multiple_of(j * tq, tq)
            return kbf[pl.ds(off, tq), :], vbf[pl.ds(off, tq), :], mask

        def visit(carry, *blocks):
            return _sb_tiles(q_heads, biases, blocks, carry, neg_upper)

        c0 = jnp.zeros((rows, 1), F32)
        a0 = jnp.zeros((rows, 2 * hd), F32)
        carry = visit(((c0, a0), (c0, a0)), block(qi, diag_mask))
        carry = lax.cond(qi % 2 == 1, lambda cr: visit(cr, block(qi - 1, None)), lambda cr: cr, carry)
        top = qi - 1 - qi % 2
        carry = lax.fori_loop(
            0, qi // 2, lambda i, cr: visit(cr, block(top - 2 * i, None), block(top - 2 * i - 1, None)), carry)
        o_ref[0, 0:rows, :] = jnp.where(lane < hd, carry[0][1], carry[1][1]).astype(o_ref.dtype)

    tail = seq - (nqb - 1) * tq
    if tail == tq:
        run(tq)
    else:
        pl.when(qi < nqb - 1)(lambda: run(tq))
        pl.when(qi == nqb - 1)(lambda: run(tail))


def sb_prompt_attention(q16, k32, v32, sb_bias, name):
    bsz, seq, d = q16.shape
    tq = SB_TQ
    nqb = pl.cdiv(seq, tq)
    assert (seq - (nqb - 1) * tq) % SUB_BF16 == 0
    w2 = 2 * SB_HEAD_DIM
    kern = functools.partial(_sb_prompt_kernel, seq=seq, tq=tq, nqb=nqb)
    return pl.pallas_call(
        kern,
        grid_spec=pltpu.PrefetchScalarGridSpec(
            num_scalar_prefetch=1,
            grid=(bsz, d // w2, nqb),
            in_specs=[pl.BlockSpec((1, tq, w2), lambda b, p, i, s: (b, i, p)),
                      pl.BlockSpec((1, seq, w2), lambda b, p, i, s: (b, 0, p)),
                      pl.BlockSpec((1, seq, w2), lambda b, p, i, s: (b, 0, p))],
            out_specs=pl.BlockSpec((1, tq, w2), lambda b, p, i, s: (b, i, p)),
            scratch_shapes=[pltpu.VMEM((nqb * tq, w2), BF16), pltpu.VMEM((nqb * tq, w2), BF16)]),
        out_shape=jax.ShapeDtypeStruct((bsz, seq, d), BF16),
        compiler_params=_params(("arbitrary", "arbitrary", "arbitrary")),
        name=name,
    )(sb_bias, q16, k32, v32)


def _sb_sample_kernel(pt_ref, bias_ref, q_ref, kn_ref, vn_ref, *rest, pps, t_new, page):
    k_pages = rest[:pps]
    v_pages = rest[pps:2 * pps]
    o_ref = rest[2 * pps]
    qbd_ref, bias_mat, c_ref, acc_ref = rest[2 * pps + 1:]
    s = pl.program_id(1)
    hd = SB_HEAD_DIM
    d = SB_HEADS * hd
    rows = SB_HEADS * t_new
    neg_upper = _neg_strict_upper(page)

    def visit(blocks, keys_on_lanes):
        ((c, acc),) = _sb_tiles([qbd_ref[...]], [bias_mat[...]], blocks, [(c_ref[...], acc_ref[...])], neg_upper,
                                keys_on_lanes)
        c_ref[...] = c
        acc_ref[...] = acc

    @pl.when(s == 0)
    def _():
        q = q_ref[0] * (hd ** -0.5)
        qrep = jnp.concatenate([q] * SB_HEADS, axis=0)
        rh = lax.broadcasted_iota(jnp.int32, (rows, d), 0) // t_new
        ch = lax.broadcasted_iota(jnp.int32, (rows, d), 1) // hd
        qbd_ref[...] = jnp.where(rh == ch, qrep, 0.0).astype(BF16)
        rh2 = lax.broadcasted_iota(jnp.int32, (rows, page), 0) // t_new
        bm = jnp.zeros((rows, page), F32)
        for h in range(SB_HEADS):
            bm = jnp.where(rh2 == h, bias_ref[h], bm)
        bias_mat[...] = bm
        c_ref[...] = jnp.zeros_like(c_ref)
        acc_ref[...] = jnp.zeros_like(acc_ref)
        zpad = jnp.zeros((page - t_new, d), F32)
        kn = jnp.concatenate([kn_ref[0], zpad], axis=0).astype(BF16)
        vn = jnp.concatenate([vn_ref[0], zpad], axis=0).astype(BF16)
        tq = lax.broadcasted_iota(jnp.int32, (rows, page), 0) % t_new
        sk = lax.broadcasted_iota(jnp.int32, (rows, page), 1)
        visit([(kn, vn, sk < tq)], False)

    visit([(k_pages[r][0].astype(BF16), v_pages[r][0].astype(BF16), None) for r in range(pps)], True)

    @pl.when(s == pl.num_programs(1) - 1)
    def _():
        acc = acc_ref[...]
        ch = lax.broadcasted_iota(jnp.int32, (t_new, d), 1) // hd
        out = jnp.zeros((t_new, d), F32)
        for h in range(SB_HEADS):
            out = out + jnp.where(ch == h, acc[h * t_new:(h + 1) * t_new, :], 0.0)
        o_ref[0] = out


def sb_sample_attention(q, k_new, v_new, cache_k, cache_v, page_table, sb_bias, name):
    bsz, t_new, d = q.shape
    page = cache_k.shape[2]
    n_pages = page_table.shape[1]
    assert page == LANES and SB_HEADS * t_new == LANES
    pps = _pick(n_pages, (8, 4, 2, 1))
    nstep = n_pages // pps
    pt = page_table.reshape(-1).astype(jnp.int32)

    def page_map(r):
        return lambda b, s, pt_ref, bias_ref: (pt_ref[b * n_pages + (n_pages - 1 - (s * pps + r))], 0, 0)

    tok = pl.BlockSpec((1, t_new, d), lambda b, s, *_: (b, 0, 0))
    page_specs = [pl.BlockSpec((1, d, page), page_map(r)) for r in range(pps)]
    kern = functools.partial(_sb_sample_kernel, pps=pps, t_new=t_new, page=page)
    return pl.pallas_call(
        kern,
        grid_spec=pltpu.PrefetchScalarGridSpec(
            num_scalar_prefetch=2,
            grid=(bsz, nstep),
            in_specs=[tok, tok, tok] + page_specs + page_specs,
            out_specs=tok,
            scratch_shapes=[pltpu.VMEM((LANES, d), BF16), pltpu.VMEM((LANES, page), F32),
                            pltpu.VMEM((LANES, page), F32), pltpu.VMEM((LANES, d), F32)]),
        out_shape=jax.ShapeDtypeStruct((bsz, t_new, d), F32),
        compiler_params=_params(("arbitrary", "arbitrary")),
        name=name,
    )(pt, sb_bias, q, k_new, v_new, *([cache_k] * pps), *([cache_v] * pps))


def _gate_kernel(x_ref, wl_ref, w2_ref, bg_ref, o_ref):
    g_low = _dot(x_ref[...].astype(BF16), wl_ref[...].astype(BF16))
    a_hi, a_lo = _split2(g_low)
    w_hi, w_lo = _split2(w2_ref[...])
    pre = _dot(a_hi, w_hi) + _dot(a_hi, w_lo) + _dot(a_lo, w_hi) + bg_ref[...]
    o_ref[...] = -_softplus(-pre) * (1.0 / GLA_TAU)


def gla_log_gate(x, w_low, w_gate2, b_gate, name):
    m, k = x.shape
    rank, kd = w_gate2.shape
    wl = jnp.pad(w_low, ((0, 0), (0, LANES - rank)))
    w2 = jnp.pad(w_gate2, ((0, LANES - rank), (0, 0)))
    tm = _pick(m, (2064, 1024, 512, 256, 128, 16))
    row = lambda i: (i, 0)
    fix = lambda i: (0, 0)
    return pl.pallas_call(
        _gate_kernel,
        grid=(m // tm,),
        in_specs=[pl.BlockSpec((tm, k), row), pl.BlockSpec((k, LANES), fix), pl.BlockSpec((LANES, kd), fix),
                  pl.BlockSpec((1, kd), fix)],
        out_specs=pl.BlockSpec((tm, kd), row),
        out_shape=jax.ShapeDtypeStruct((m, kd), F32),
        compiler_params=_params(("arbitrary",)),
        name=name,
    )(x, wl, w2, b_gate.reshape(1, kd))


def _gla_chunk(q, k, v, lg, state):
    c, dk = q.shape
    dv = v.shape[1]
    sub = GLA_SUB
    nsub = c // sub
    ri = lax.broadcasted_iota(jnp.int32, (c, c), 0)
    ci = lax.broadcasted_iota(jnp.int32, (c, c), 1)
    lower = jnp.where(ri >= ci, 1.0, 0.0).astype(BF16)
    ones_c = jnp.ones((c, dk), BF16)
    parts = _split3(lg)
    b = _dot(lower, parts[0]) + _dot(lower, parts[1]) + _dot(lower, parts[2])
    b_end_col = _dot_tn(parts[0], ones_c) + _dot_tn(parts[1], ones_c) + _dot_tn(parts[2], ones_c)
    b_end = b[c - 1:c, :]

    o = _dot((q * jnp.exp(b)).astype(BF16), state.astype(BF16))
    k_end = (k * jnp.exp(b_end - b)).astype(BF16)
    decay = jnp.exp(b_end_col)
    new_state = jnp.concatenate([decay] * (dv // dk), axis=1) * state + _dot_tn(k_end, v)

    row16 = lax.broadcasted_iota(jnp.int32, (sub, dk), 0)
    lane16 = lax.broadcasted_iota(jnp.int32, (sub, LANES), 1)
    rowc = lax.broadcasted_iota(jnp.int32, (c, dk), 0)
    pieces = []
    for i in range(nsub):
        qs = q[i * sub:(i + 1) * sub]
        ks = k[i * sub:(i + 1) * sub]
        bs = b[i * sub:(i + 1) * sub]
        for s in range(sub):
            e = jnp.exp(jnp.where(row16 >= s, bs - bs[s:s + 1, :], -1e30))
            pieces.append((qs * e * ks[s:s + 1, :]).astype(BF16))
    pair = _dot(jnp.concatenate(pieces, axis=0), jnp.ones((dk, LANES), BF16))
    zrows = jnp.zeros((LANES - c, dk), BF16) if c < LANES else None
    score_rows = []
    for i in range(nsub):
        sc = jnp.zeros((sub, LANES), F32)
        for s in range(sub):
            blk = pair[(i * sub + s) * sub:(i * sub + s + 1) * sub]
            sc = jnp.where(lane16 == i * sub + s, blk, sc)
        if i > 0:
            b_ref = b[i * sub - 1:i * sub, :]
            qq = (q[i * sub:(i + 1) * sub] * jnp.exp(b[i * sub:(i + 1) * sub] - b_ref)).astype(BF16)
            kk = jnp.where(rowc < i * sub, k * jnp.exp(jnp.minimum(b_ref - b, 0.0)), 0.0).astype(BF16)
            if zrows is not None:
                kk = jnp.concatenate([kk, zrows], axis=0)
            sc = sc + _dot_nt(qq, kk)
        score_rows.append(sc)
    scores = jnp.concatenate(score_rows, axis=0).astype(BF16)
    vpad = v if c == LANES else jnp.concatenate([v, jnp.zeros((LANES - c, dv), BF16)], axis=0)
    o = o + _dot(scores, vpad)
    return o, new_state


def _gla_finish(o, r, gn):
    ms = jnp.mean(o * o, axis=-1, keepdims=True)
    return o * lax.rsqrt(ms + RMS_EPS) * gn * (r * _sigmoid(r))


def _gla_prompt_kernel(q_ref, k_ref, lg_ref, v_ref, r_ref, gn_ref, og_ref, s_ref, state, *, n0, chunk, nchunk):
    scale = GLA_HEAD_K ** -0.5
    gn = gn_ref[...]

    def step(start, c):
        q = q_ref[0, pl.ds(start, c), :] * scale
        o, s_new = _gla_chunk(q, k_ref[0, pl.ds(start, c), :], v_ref[0, pl.ds(start, c), :],
                              lg_ref[0, pl.ds(start, c), :], state[...])
        state[...] = s_new
        og_ref[0, pl.ds(start, c), :] = _gla_finish(o, r_ref[0, pl.ds(start, c), :].astype(F32), gn).astype(og_ref.dtype)

    state[...] = jnp.zeros_like(state)
    if n0:
        step(0, n0)

    def body(i, carry):
        step(pl.multiple_of(n0 + i * chunk, SUB_BF16), chunk)
        return carry

    lax.fori_loop(0, nchunk, body, 0)
    s_ref[0, 0] = state[...]


def gla_prompt(q, k, lg, v16, r16, gnorm_g, name):
    bsz, seq, _ = q.shape
    dk, dv, nh = GLA_HEAD_K, GLA_HEAD_V, GLA_HEADS
    n0 = N_META
    chunk = GLA_CHUNK
    nchunk = (seq - n0) // chunk
    assert n0 + nchunk * chunk == seq and n0 % GLA_SUB == 0
    kspec = pl.BlockSpec((1, seq, dk), lambda b, h: (b, 0, h))
    vspec = pl.BlockSpec((1, seq, dv), lambda b, h: (b, 0, h))
    kern = functools.partial(_gla_prompt_kernel, n0=n0, chunk=chunk, nchunk=nchunk)
    return pl.pallas_call(
        kern,
        grid=(bsz, nh),
        in_specs=[kspec, kspec, kspec, vspec, vspec, pl.BlockSpec((1, dv), lambda b, h: (0, 0))],
        out_specs=[vspec, pl.BlockSpec((1, 1, dk, dv), lambda b, h: (b, h, 0, 0))],
        out_shape=[jax.ShapeDtypeStruct((bsz, seq, nh * dv), BF16),
                   jax.ShapeDtypeStruct((bsz, nh, dk, dv), F32)],
        scratch_shapes=[pltpu.VMEM((dk, dv), F32)],
        compiler_params=_params(("arbitrary", "arbitrary")),
        name=name,
    )(q, k, lg, v16, r16, gnorm_g.reshape(1, dv))


def _gla_sample_kernel(q_ref, k_ref, lg_ref, v_ref, r_ref, gn_ref, s0_ref, og_ref, s_ref, *, t_new):
    scale = GLA_HEAD_K ** -0.5
    dk, dv = GLA_HEAD_K, GLA_HEAD_V
    gn = gn_ref[...]
    padk = jnp.zeros((GLA_SUB - t_new, dk), F32)
    padv = jnp.zeros((GLA_SUB - t_new, dv), F32)
    for h in range(GLA_HEADS):
        ks = slice(h * dk, (h + 1) * dk)
        vs = slice(h * dv, (h + 1) * dv)
        q = jnp.concatenate([q_ref[0, :, ks] * scale, padk], axis=0)
        k = jnp.concatenate([k_ref[0, :, ks], padk], axis=0)
        lg = jnp.concatenate([lg_ref[0, :, ks], padk], axis=0)
        v = jnp.concatenate([v_ref[0, :, vs], padv], axis=0).astype(BF16)
        o, s_new = _gla_chunk(q, k, v, lg, s0_ref[0, h])
        s_ref[0, h] = s_new
        og_ref[0, :, vs] = _gla_finish(o[0:t_new], r_ref[0, :, vs], gn)


def gla_sample(q, k, lg, v, r, gnorm_g, state, name):
    bsz, t_new, _ = q.shape
    dk, dv, nh = GLA_HEAD_K, GLA_HEAD_V, GLA_HEADS
    assert t_new <= GLA_SUB
    kspec = pl.BlockSpec((1, t_new, nh * dk), lambda b: (b, 0, 0))
    vspec = pl.BlockSpec((1, t_new, nh * dv), lambda b: (b, 0, 0))
    sspec = pl.BlockSpec((1, nh, dk, dv), lambda b: (b, 0, 0, 0))
    kern = functools.partial(_gla_sample_kernel, t_new=t_new)
    return pl.pallas_call(
        kern,
        grid=(bsz,),
        in_specs=[kspec, kspec, kspec, vspec, vspec, pl.BlockSpec((1, dv), lambda b: (0, 0)), sspec],
        out_specs=[vspec, sspec],
        out_shape=[jax.ShapeDtypeStruct((bsz, t_new, nh * dv), F32),
                   jax.ShapeDtypeStruct((bsz, nh, dk, dv), F32)],
        compiler_params=_params(("arbitrary",)),
        name=name,
    )(q, k, lg, v, r, gnorm_g.reshape(1, dv), state)


def _router_kernel(h_ref, wr_ref, meta_ref, cnt_ref, carry, *, n_tok, tm):
    i = pl.program_id(0)

    @pl.when(i == 0)
    def _():
        carry[...] = jnp.zeros_like(carry)

    hrow = lax.broadcasted_iota(jnp.int32, h_ref.shape, 0)
    h_hi, h_lo = _split2(jnp.where(i * tm + hrow < n_tok, h_ref[...], 0.0))
    w_hi, w_lo = _split2(wr_ref[...])
    logits = _dot(h_hi, w_hi) + _dot(h_lo, w_hi) + _dot(h_hi, w_lo)
    lane = lax.broadcasted_iota(jnp.int32, (tm, LANES), 1)
    row = lax.broadcasted_iota(jnp.int32, (tm, LANES), 0)
    neg = -jnp.inf
    l1 = jnp.where(lane < N_EXPERTS, logits, neg)
    m1 = jnp.max(l1, axis=1, keepdims=True)
    i1 = jnp.min(jnp.where(l1 == m1, lane, LANES), axis=1, keepdims=True)
    l2 = jnp.where(lane == i1, neg, l1)
    m2 = jnp.max(l2, axis=1, keepdims=True)
    i2 = jnp.min(jnp.where(l2 == m2, lane, LANES), axis=1, keepdims=True)
    e = jnp.exp(m2 - m1)
    w1 = 1.0 / (1.0 + e)
    w2 = e * w1
    valid = (i * tm + row) < n_tok
    hit = jnp.where(valid & ((lane == i1) | (lane == i2)), 1.0, 0.0)
    ri = lax.broadcasted_iota(jnp.int32, (tm, tm), 0)
    ci = lax.broadcasted_iota(jnp.int32, (tm, tm), 1)
    before = jnp.where(ri > ci, 1.0, 0.0).astype(BF16)
    rank = _dot(before, hit.astype(BF16)) + carry[0:1, :]
    r1 = jnp.sum(jnp.where(lane == i1, rank, 0.0), axis=1, keepdims=True)
    r2 = jnp.sum(jnp.where(lane == i2, rank, 0.0), axis=1, keepdims=True)
    carry[0:1, :] = carry[0:1, :] + jnp.sum(hit, axis=0, keepdims=True)
    meta = jnp.where(lane == 0, i1.astype(F32), 0.0)
    meta = jnp.where(lane == 1, i2.astype(F32), meta)
    meta = jnp.where(lane == 2, w1, meta)
    meta = jnp.where(lane == 3, w2, meta)
    meta = jnp.where(lane == 4, r1, meta)
    meta = jnp.where(lane == 5, r2, meta)
    meta_ref[...] = meta

    @pl.when(i == pl.num_programs(0) - 1)
    def _():
        cnt_ref[...] = carry[...]


def moe_route(h32, w_router):
    n_tok, d = h32.shape
    tm = ROUTE_TM
    nt = pl.cdiv(n_tok, tm)
    wr = jnp.pad(w_router, ((0, 0), (0, LANES - w_router.shape[1])))
    kern = functools.partial(_router_kernel, n_tok=n_tok, tm=tm)
    return pl.pallas_call(
        kern,
        grid=(nt,),
        in_specs=[pl.BlockSpec((tm, d), lambda i: (i, 0)), pl.BlockSpec((d, LANES), lambda i: (0, 0))],
        out_specs=[pl.BlockSpec((tm, LANES), lambda i: (i, 0)), pl.BlockSpec((8, LANES), lambda i: (0, 0))],
        out_shape=[jax.ShapeDtypeStruct((nt * tm, LANES), F32), jax.ShapeDtypeStruct((8, LANES), F32)],
        scratch_shapes=[pltpu.VMEM((8, LANES), F32)],
        compiler_params=_params(("arbitrary",)),
        name="moe_route",
    )(h32, wr)


def _row_copy(src, dst, i, j, sem):
    return pltpu.make_async_copy(src.at[pl.ds(i, 1)], dst.at[pl.ds(j, 1)], sem)


def _dispatch_kernel(pos_ref, h_ref, xs_in, xs_out, sem, *, tm):
    del xs_in

    def for_each(fn):
        def body(r, carry):
            for slot in range(2):
                p = pos_ref[0, 0, 2 * r + slot]

                @pl.when(p >= 0)
                def _():
                    fn(_row_copy(h_ref, xs_out, r, p, sem))
            return carry
        lax.fori_loop(0, tm, body, 0)

    for_each(lambda cp: cp.start())
    for_each(lambda cp: cp.wait())


def moe_dispatch(h32, pos, n_rows):
    n_tok, d = h32.shape
    tm = ROUTE_TM
    nt = pl.cdiv(n_tok, tm)
    pos = jnp.pad(pos, ((0, nt * tm - n_tok), (0, 0)), constant_values=-1).reshape(nt, 1, 2 * tm)
    kern = functools.partial(_dispatch_kernel, tm=tm)
    any_spec = pl.BlockSpec(memory_space=pl.ANY)
    return pl.pallas_call(
        kern,
        grid=(nt,),
        in_specs=[pl.BlockSpec((1, 1, 2 * tm), lambda i: (i, 0, 0), memory_space=pltpu.SMEM),
                  pl.BlockSpec((tm, d), lambda i: (i, 0)), any_spec],
        out_specs=any_spec,
        out_shape=jax.ShapeDtypeStruct((n_rows, d), F32),
        scratch_shapes=[pltpu.SemaphoreType.DMA(())],
        input_output_aliases={2: 0},
        compiler_params=_params(("arbitrary",)),
        name="moe_dispatch",
    )(pos, h32, jnp.zeros((n_rows, d), F32))


def _experts_kernel(te_ref, tv_ref, x_ref, wg_ref, wu_ref, wd_ref, y_ref, xbf, acc):
    t = pl.program_id(0)
    f = pl.program_id(1)
    live = tv_ref[t] > 0

    @pl.when(f == 0)
    def _():
        xbf[...] = x_ref[...].astype(BF16)
        acc[...] = jnp.zeros_like(acc)

    @pl.when(live)
    def _():
        x = xbf[...]
        gate = _dot(x, wg_ref[0].astype(BF16))
        up = _dot(x, wu_ref[0].astype(BF16))
        act = (gate * _sigmoid(gate) * up).astype(BF16)
        acc[...] += _dot(act, wd_ref[0].astype(BF16))

    @pl.when(f == pl.num_programs(1) - 1)
    def _():
        y_ref[...] = acc[...]


def moe_experts(xs, w_gu_e, w_down_e, tile_expert, tile_live):
    n_rows, d = xs.shape
    dff = w_down_e.shape[1]
    tm, tf = MOE_TM, MOE_TF
    nf = dff // tf
    assert nf * tf == dff and n_rows % tm == 0

    def fcol(t, f, tv):
        return jnp.where(tv[t] > 0, f, nf - 1)

    return pl.pallas_call(
        _experts_kernel,
        grid_spec=pltpu.PrefetchScalarGridSpec(
            num_scalar_prefetch=2,
            grid=(n_rows // tm, nf),
            in_specs=[pl.BlockSpec((tm, d), lambda t, f, te, tv: (t, 0)),
                      pl.BlockSpec((1, d, tf), lambda t, f, te, tv: (te[t], 0, fcol(t, f, tv))),
                      pl.BlockSpec((1, d, tf), lambda t, f, te, tv: (te[t], 0, nf + fcol(t, f, tv))),
                      pl.BlockSpec((1, tf, d), lambda t, f, te, tv: (te[t], fcol(t, f, tv), 0))],
            out_specs=pl.BlockSpec((tm, d), lambda t, f, te, tv: (t, 0)),
            scratch_shapes=[pltpu.VMEM((tm, d), BF16), pltpu.VMEM((tm, d), F32)]),
        out_shape=jax.ShapeDtypeStruct((n_rows, d), F32),
        compiler_params=_params(("arbitrary", "arbitrary")),
        name="moe_experts",
    )(tile_expert, tile_live, xs, w_gu_e, w_gu_e, w_down_e)


def _combine_kernel(pos_ref, meta_ref, res_ref, g_ref, b_ref, y_hbm, o32_ref, o16_ref, y1, y2, sem, *, tm):
    def for_each(fn):
        def body(r, carry):
            fn(_row_copy(y_hbm, y1, pos_ref[0, 0, 2 * r], r, sem))
            fn(_row_copy(y_hbm, y2, pos_ref[0, 0, 2 * r + 1], r, sem))
            return carry
        lax.fori_loop(0, tm, body, 0)

    for_each(lambda cp: cp.start())
    for_each(lambda cp: cp.wait())
    meta = meta_ref[...]
    mix = meta[:, 2:3] * y1[...] + meta[:, 3:4] * y2[...]
    h = _layer_norm(DEEPNORM_ALPHA * res_ref[...] + mix, g_ref[...], b_ref[...])
    o32_ref[...] = h
    o16_ref[...] = h.astype(BF16)


def moe_combine_deepnorm(ys, pos, meta, res, g, b, name):
    m, d = res.shape
    tm = _pick(m, (688, 512, 256, 128, 16))
    nt = m // tm
    row = lambda i: (i, 0)
    fix = lambda i: (0, 0)
    kern = functools.partial(_combine_kernel, tm=tm)
    return pl.pallas_call(
        kern,
        grid=(nt,),
        in_specs=[pl.BlockSpec((1, 1, 2 * tm), lambda i: (i, 0, 0), memory_space=pltpu.SMEM),
                  pl.BlockSpec((tm, LANES), row), pl.BlockSpec((tm, d), row),
                  pl.BlockSpec((1, d), fix), pl.BlockSpec((1, d), fix), pl.BlockSpec(memory_space=pl.ANY)],
        out_specs=[pl.BlockSpec((tm, d), row), pl.BlockSpec((tm, d), row)],
        out_shape=[jax.ShapeDtypeStruct((m, d), F32), jax.ShapeDtypeStruct((m, d), BF16)],
        scratch_shapes=[pltpu.VMEM((tm, d), F32), pltpu.VMEM((tm, d), F32), pltpu.SemaphoreType.DMA(())],
        compiler_params=_params(("arbitrary",)),
        name=name,
    )(pos.reshape(nt, 1, 2 * tm), meta, res, g.reshape(1, d), b.reshape(1, d), ys)


def moe_deepnorm(hp32, hs32, w_router, w_gu_e, w_down_e, g, b, layer):
    mp = hp32.shape[0]
    h_all = jnp.concatenate([hp32, hs32], axis=0)
    n_tok = h_all.shape[0]
    meta, cnt = moe_route(h_all, w_router)
    meta = meta[:n_tok]
    counts = cnt[0, :N_EXPERTS].astype(jnp.int32)
    padded = ((counts + MOE_TM - 1) // MOE_TM) * MOE_TM
    ends = jnp.cumsum(padded)
    starts = ends - padded
    e1 = meta[:, 0].astype(jnp.int32)
    e2 = meta[:, 1].astype(jnp.int32)
    eids = jnp.arange(N_EXPERTS, dtype=jnp.int32)
    start1 = jnp.sum(jnp.where(e1[:, None] == eids[None, :], starts[None, :], 0), axis=1)
    start2 = jnp.sum(jnp.where(e2[:, None] == eids[None, :], starts[None, :], 0), axis=1)
    pos = jnp.stack([start1 + meta[:, 4].astype(jnp.int32), start2 + meta[:, 5].astype(jnp.int32)], axis=1)
    n_tiles = pl.cdiv(2 * n_tok, MOE_TM) + N_EXPERTS
    tile_start = jnp.arange(n_tiles, dtype=jnp.int32) * MOE_TM
    tile_live = (tile_start < ends[-1]).astype(jnp.int32)
    tile_expert = jnp.sum((tile_start[:, None] >= ends[None, :]).astype(jnp.int32), axis=1)
    last_live = jnp.sum((jnp.maximum(ends[-1] - 1, 0) >= ends).astype(jnp.int32))
    tile_expert = jnp.where(tile_live > 0, tile_expert, last_live).astype(jnp.int32)

    xs = moe_dispatch(h_all, pos, n_tiles * MOE_TM)
    ys = moe_experts(xs, w_gu_e, w_down_e, tile_expert, tile_live)
    outp = moe_combine_deepnorm(ys, pos[:mp], meta[:mp], hp32, g, b, f"l{layer}_moe_combine_p")
    outs = moe_combine_deepnorm(ys, pos[mp:], meta[mp:], hs32, g, b, f"l{layer}_moe_combine_s")
    return outp, outs


def _sb_layer(i, hp32, hp16, hs32, hs16, bsz, seq, dec_b, dec_t, cache_k, cache_v, page_table,
              w_qkv, w_o, sb_bias, ln1_g, ln1_b):
    d = D_MODEL
    tag = f"l{i}_"
    qp = linear(hp16, w_qkv, 0, d, BF16, tag + "q_p")
    kp = linear(hp16, w_qkv, d, d, F32, tag + "k_p")
    vp = linear(hp16, w_qkv, 2 * d, d, F32, tag + "v_p")
    qs = linear(hs16, w_qkv, 0, d, F32, tag + "q_s")
    ks = linear(hs16, w_qkv, d, d, F32, tag + "k_s")
    vs = linear(hs16, w_qkv, 2 * d, d, F32, tag + "v_s")
    op = sb_prompt_attention(qp.reshape(bsz, seq, d), kp.reshape(bsz, seq, d), vp.reshape(bsz, seq, d),
                             sb_bias, tag + "sb_p")
    pool, page = cache_k.shape[0], cache_k.shape[1]
    ck = jnp.transpose(cache_k, (0, 2, 3, 1)).reshape(pool, d, page)
    cv = jnp.transpose(cache_v, (0, 2, 3, 1)).reshape(pool, d, page)
    os_ = sb_sample_attention(qs.reshape(dec_b, dec_t, d), ks.reshape(dec_b, dec_t, d), vs.reshape(dec_b, dec_t, d),
                              ck, cv, page_table, sb_bias, tag + "sb_s")
    hp32, hp16 = linear_deepnorm(op.reshape(bsz * seq, d), w_o, hp32, ln1_g, ln1_b, tag + "o_p")
    hs32, hs16 = linear_deepnorm(os_.reshape(dec_b * dec_t, d), w_o, hs32, ln1_g, ln1_b, tag + "o_s")
    shp = (bsz, seq, SB_HEADS, SB_HEAD_DIM)
    shs = (dec_b, dec_t, SB_HEADS, SB_HEAD_DIM)
    new_state = [kp.reshape(shp), vp.reshape(shp), ks.reshape(shs), vs.reshape(shs)]
    return hp32, hp16, hs32, hs16, new_state


def _gla_layer(i, hp32, hp16, hs32, hs16, bsz, seq, dec_b, dec_t, state,
               w_in, w_gate2, b_gate, gnorm_g, w_o, ln1_g, ln1_b):
    tag = f"l{i}_"
    kd, vd = GLA_KEY_DIM, GLA_HEADS * GLA_HEAD_V
    w_low = w_in[:, 2 * kd + 2 * vd:]

    def project(x16, vdtype, sfx):
        q = linear(x16, w_in, 0, kd, F32, tag + "q" + sfx)
        k = linear(x16, w_in, kd, kd, F32, tag + "k" + sfx)
        v = linear(x16, w_in, 2 * kd, vd, vdtype, tag + "v" + sfx)
        r = linear(x16, w_in, 2 * kd + vd, vd, vdtype, tag + "r" + sfx)
        lg = gla_log_gate(x16, w_low, w_gate2, b_gate, tag + "gate" + sfx)
        return q, k, lg, v, r

    q, k, lg, v, r = project(hp16, BF16, "_p")
    og_p, sp = gla_prompt(q.reshape(bsz, seq, kd), k.reshape(bsz, seq, kd), lg.reshape(bsz, seq, kd),
                          v.reshape(bsz, seq, vd), r.reshape(bsz, seq, vd), gnorm_g, tag + "gla_p")
    q, k, lg, v, r = project(hs16, F32, "_s")
    og_s, ss = gla_sample(q.reshape(dec_b, dec_t, kd), k.reshape(dec_b, dec_t, kd), lg.reshape(dec_b, dec_t, kd),
                          v.reshape(dec_b, dec_t, vd), r.reshape(dec_b, dec_t, vd), gnorm_g, state, tag + "gla_s")
    hp32, hp16 = linear_deepnorm(og_p.reshape(bsz * seq, vd), w_o, hp32, ln1_g, ln1_b, tag + "o_p")
    hs32, hs16 = linear_deepnorm(og_s.reshape(dec_b * dec_t, vd), w_o, hs32, ln1_g, ln1_b, tag + "o_s")
    return hp32, hp16, hs32, hs16, [sp, ss]


def kernel(x_prompt, x_sample, cache_k0, cache_v0, state_s1, cache_k2, cache_v2, state_s3, page_table, meta_tokens, l0_w_qkv, l0_w_o, l0_sb_bias, l0_ln1_g, l0_ln1_b, l0_w_gu, l0_w_down, l0_ln2_g, l0_ln2_b, l1_w_in, l1_w_gate2, l1_b_gate, l1_gnorm_g, l1_w_o, l1_ln1_g, l1_ln1_b, l1_w_router, l1_w_gu_e, l1_w_down_e, l1_ln2_g, l1_ln2_b, l2_w_qkv, l2_w_o, l2_sb_bias, l2_ln1_g, l2_ln1_b, l2_w_gu, l2_w_down, l2_ln2_g, l2_ln2_b, l3_w_in, l3_w_gate2, l3_b_gate, l3_gnorm_g, l3_w_o, l3_ln1_g, l3_ln1_b, l3_w_router, l3_w_gu_e, l3_w_down_e, l3_ln2_g, l3_ln2_b):
    bsz, seq0, d = x_prompt.shape
    dec_b, dec_t, _ = x_sample.shape
    n_meta = meta_tokens.shape[0]
    seq = n_meta + seq0
    meta = jnp.broadcast_to(meta_tokens.astype(x_prompt.dtype)[None], (bsz, n_meta, d))
    hp32 = jnp.concatenate([meta, x_prompt], axis=1).reshape(bsz * seq, d)
    hs32 = x_sample.reshape(dec_b * dec_t, d)
    hp16 = hp32.astype(BF16)
    hs16 = hs32.astype(BF16)

    sb = [(cache_k0, cache_v0, l0_w_qkv, l0_w_o, l0_sb_bias, l0_ln1_g, l0_ln1_b),
          (cache_k2, cache_v2, l2_w_qkv, l2_w_o, l2_sb_bias, l2_ln1_g, l2_ln1_b)]
    ffn = [(l0_w_gu, l0_w_down, l0_ln2_g, l0_ln2_b), (l2_w_gu, l2_w_down, l2_ln2_g, l2_ln2_b)]
    gla = [(state_s1, l1_w_in, l1_w_gate2, l1_b_gate, l1_gnorm_g, l1_w_o, l1_ln1_g, l1_ln1_b),
           (state_s3, l3_w_in, l3_w_gate2, l3_b_gate, l3_gnorm_g, l3_w_o, l3_ln1_g, l3_ln1_b)]
    moe = [(l1_w_router, l1_w_gu_e, l1_w_down_e, l1_ln2_g, l1_ln2_b),
           (l3_w_router, l3_w_gu_e, l3_w_down_e, l3_ln2_g, l3_ln2_b)]

    new_state = []
    for i in range(DEPTH):
        if i % 2 == 0:
            cache_k, cache_v, w_qkv, w_o, sb_bias, g1, b1 = sb[i // 2]
            hp32, hp16, hs32, hs16, st = _sb_layer(i, hp32, hp16, hs32, hs16, bsz, seq, dec_b, dec_t,
                                                   cache_k, cache_v, page_table, w_qkv, w_o, sb_bias, g1, b1)
            w_gu, w_down, g2, b2 = ffn[i // 2]
            hp32, hp16 = swiglu_deepnorm(hp16, w_gu, w_down, hp32, g2, b2, f"l{i}_ffn_p")
            hs32, hs16 = swiglu_deepnorm(hs16, w_gu, w_down, hs32, g2, b2, f"l{i}_ffn_s")
        else:
            state, w_in, w_gate2, b_gate, gnorm_g, w_o, g1, b1 = gla[i // 2]
            hp32, hp16, hs32, hs16, st = _gla_layer(i, hp32, hp16, hs32, hs16, bsz, seq, dec_b, dec_t, state,
                                                    w_in, w_gate2, b_gate, gnorm_g, w_o, g1, b1)
            w_router, w_gu_e, w_down_e, g2, b2 = moe[i // 2]
            (hp32, hp16), (hs32, hs16) = moe_deepnorm(hp32, hs32, w_router, w_gu_e, w_down_e, g2, b2, i)
        new_state += st

    y_prompt = hp32.reshape(bsz, seq, d)[:, n_meta:]
    y_sample = hs32.reshape(dec_b, dec_t, d)
    return (y_prompt, y_sample, *new_state)
```

```python
import functools

import jax
import jax.numpy as jnp
from jax import lax
from jax.experimental import pallas as pl
from jax.experimental.pallas import tpu as pltpu

F32 = jnp.float32
BF16 = jnp.bfloat16

D_MODEL = 1024
N_META = 16
SB_HEADS = 16
SB_HEAD_DIM = 64
GLA_HEADS = 4
GLA_HEAD_K = 128
GLA_HEAD_V = 256
GLA_KEY_DIM = GLA_HEADS * GLA_HEAD_K
GLA_GATE_RANK = 16
GLA_TAU = 16.0
N_EXPERTS = 8
DEPTH = 4
DEEPNORM_ALPHA = float((2 * DEPTH) ** 0.25)
LN_EPS = 1e-5
RMS_EPS = 1e-6

LANES = 128
SUB_BF16 = 16
VMEM_LIMIT = 56 * 1024 * 1024

SB_TQ = 256
SB_GROUP = 4
GLA_SUB = 16
GLA_CHUNKS = (48, 64, 32, 16)
MOE_TM = 1024
MOE_TF = 512
ROUTE_TM = 1024
DMA_UNROLL = 4
FFN_TF = 256


def _params(sem):
    return pltpu.CompilerParams(dimension_semantics=sem, vmem_limit_bytes=VMEM_LIMIT)


def _row_tile(m, cap):
    for t in range(cap - cap % 8, 0, -8):
        if m % t == 0:
            return t
    raise ValueError(f"no row tile for {m}")


def _pick(m, cands):
    for c in cands:
        if m % c == 0:
            return c
    raise ValueError(f"no tile for {m} in {cands}")


def _dot(a, b):
    return jnp.dot(a, b, preferred_element_type=F32)


def _dot_nt(a, b):
    return lax.dot_general(a, b, (((1,), (1,)), ((), ())), preferred_element_type=F32)


def _dot_tn(a, b):
    return lax.dot_general(a, b, (((0,), (0,)), ((), ())), preferred_element_type=F32)


def _softplus(z):
    return jnp.maximum(z, 0.0) + jnp.log1p(jnp.exp(-jnp.abs(z)))


def _sigmoid(x):
    return 1.0 / (1.0 + jnp.exp(-x))


def _split2(x):
    hi = x.astype(BF16)
    lo = (x - hi.astype(F32)).astype(BF16)
    return hi, lo


def _split3(x):
    a = x.astype(BF16)
    r = x - a.astype(F32)
    b = r.astype(BF16)
    c = (r - b.astype(F32)).astype(BF16)
    return a, b, c


def _layer_norm(y, g, b):
    mu = jnp.mean(y, axis=-1, keepdims=True)
    yc = y - mu
    var = jnp.mean(yc * yc, axis=-1, keepdims=True)
    return yc * lax.rsqrt(var + LN_EPS) * g + b


def _linear_kernel(x_ref, w_ref, o_ref, wbf_ref):
    @pl.when(pl.program_id(1) == 0)
    def _():
        wbf_ref[...] = w_ref[...].astype(BF16)

    o_ref[...] = _dot(x_ref[...].astype(BF16), wbf_ref[...]).astype(o_ref.dtype)


def linear(x, w, col0, ncols, out_dtype, name):
    m, k = x.shape
    tm = _pick(m, (2064, 1024, 512, 256, 128, 16))
    tn = _pick(ncols, (512, 256, 128))
    assert col0 % tn == 0
    return pl.pallas_call(
        _linear_kernel,
        grid=(ncols // tn, m // tm),
        in_specs=[pl.BlockSpec((tm, k), lambda j, i: (i, 0)),
                  pl.BlockSpec((k, tn), lambda j, i: (0, col0 // tn + j))],
        out_specs=pl.BlockSpec((tm, tn), lambda j, i: (i, j)),
        out_shape=jax.ShapeDtypeStruct((m, ncols), out_dtype),
        scratch_shapes=[pltpu.VMEM((k, tn), BF16)],
        compiler_params=_params(("arbitrary", "arbitrary")),
        name=name,
    )(x, w)


def _linear_ln_kernel(x_ref, w_ref, res_ref, g_ref, b_ref, o32_ref, o16_ref, wbf_ref):
    @pl.when(pl.program_id(0) == 0)
    def _():
        wbf_ref[...] = w_ref[...].astype(BF16)

    y = _dot(x_ref[...].astype(BF16), wbf_ref[...])
    h = _layer_norm(DEEPNORM_ALPHA * res_ref[...] + y, g_ref[...], b_ref[...])
    o32_ref[...] = h
    o16_ref[...] = h.astype(BF16)


def linear_deepnorm(x, w, res, g, b, name):
    m, k = x.shape
    d = w.shape[1]
    tm = _pick(m, (688, 512, 256, 128, 16))
    row = lambda i: (i, 0)
    fix = lambda i: (0, 0)
    return pl.pallas_call(
        _linear_ln_kernel,
        grid=(m // tm,),
        in_specs=[pl.BlockSpec((tm, k), row), pl.BlockSpec((k, d), fix), pl.BlockSpec((tm, d), row),
                  pl.BlockSpec((1, d), fix), pl.BlockSpec((1, d), fix)],
        out_specs=[pl.BlockSpec((tm, d), row), pl.BlockSpec((tm, d), row)],
        out_shape=[jax.ShapeDtypeStruct((m, d), F32), jax.ShapeDtypeStruct((m, d), BF16)],
        scratch_shapes=[pltpu.VMEM((k, d), BF16)],
        compiler_params=_params(("arbitrary",)),
        name=name,
    )(x, w, res, g.reshape(1, d), b.reshape(1, d))


def _swiglu_ln_kernel(x_ref, wg_ref, wu_ref, wd_ref, res_ref, g_ref, b_ref, o32_ref, o16_ref, acc_ref):
    f = pl.program_id(1)

    @pl.when(f == 0)
    def _():
        acc_ref[...] = jnp.zeros_like(acc_ref)

    x = x_ref[...]
    gate = _dot(x, wg_ref[...].astype(BF16))
    up = _dot(x, wu_ref[...].astype(BF16))
    act = (gate * _sigmoid(gate) * up).astype(BF16)
    acc_ref[...] += _dot(act, wd_ref[...].astype(BF16))

    @pl.when(f == pl.num_programs(1) - 1)
    def _():
        h = _layer_norm(DEEPNORM_ALPHA * res_ref[...] + acc_ref[...], g_ref[...], b_ref[...])
        o32_ref[...] = h
        o16_ref[...] = h.astype(BF16)


def swiglu_deepnorm(x16, w_gu, w_down, res, g, b, name):
    m, d = x16.shape
    dff = w_down.shape[0]
    tm = _pick(m, (688, 512, 256, 128, 16))
    tf = FFN_TF
    nf = dff // tf
    assert nf * tf == dff
    row = lambda i, f: (i, 0)
    fix = lambda i, f: (0, 0)
    return pl.pallas_call(
        _swiglu_ln_kernel,
        grid=(m // tm, nf),
        in_specs=[pl.BlockSpec((tm, d), row),
                  pl.BlockSpec((d, tf), lambda i, f: (0, f)),
                  pl.BlockSpec((d, tf), lambda i, f: (0, nf + f)),
                  pl.BlockSpec((tf, d), lambda i, f: (f, 0)),
                  pl.BlockSpec((tm, d), row), pl.BlockSpec((1, d), fix), pl.BlockSpec((1, d), fix)],
        out_specs=[pl.BlockSpec((tm, d), row), pl.BlockSpec((tm, d), row)],
        out_shape=[jax.ShapeDtypeStruct((m, d), F32), jax.ShapeDtypeStruct((m, d), BF16)],
        scratch_shapes=[pltpu.VMEM((tm, d), F32)],
        compiler_params=_params(("arbitrary", "arbitrary")),
        name=name,
    )(x16, w_gu, w_gu, w_down, res, g.reshape(1, d), b.reshape(1, d))


def _neg_strict_upper(n):
    r = lax.broadcasted_iota(jnp.int32, (n, n), 0)
    c = lax.broadcasted_iota(jnp.int32, (n, n), 1)
    return jnp.where(r > c, -1.0, 0.0).astype(BF16)


def _sb_tiles(qs, biases, blocks, carry, neg_upper, keys_on_lanes=False):
    tiles = [(h, b) for b in range(len(blocks)) for h in range(len(qs))]
    z = {}
    for h, b in tiles:
        kb = blocks[b][0]
        z[h, b] = (_dot(qs[h], kb) if keys_on_lanes else _dot_nt(qs[h], kb)) + biases[h]
    hi, lo, log_beta, c_at = {}, {}, {}, {}
    c_run = [cr[0] for cr in carry]
    for h, b in tiles:
        mask = blocks[b][2]
        sp = jnp.maximum(z[h, b], 0.0) + jnp.log(1.0 + jnp.exp(-jnp.abs(z[h, b])))
        log_beta[h, b] = z[h, b] - sp
        if mask is not None:
            sp = jnp.where(mask, sp, 0.0)
        hi[h, b], lo[h, b] = _split2(sp)
        c_at[h, b] = c_run[h]
        c_run[h] = c_run[h] - jnp.sum(sp, axis=1, keepdims=True)
    stick = {}
    upper2 = jnp.concatenate([neg_upper, neg_upper], axis=0)
    for h, b in tiles:
        stick[h, b] = _dot(jnp.concatenate([hi[h, b], lo[h, b]], axis=1), upper2)
    w = {}
    for h, b in tiles:
        mask = blocks[b][2]
        wt = jnp.exp(log_beta[h, b] + stick[h, b] + c_at[h, b])
        if mask is not None:
            wt = jnp.where(mask, wt, 0.0)
        w[h, b] = wt.astype(BF16)
    acc = [cr[1] for cr in carry]
    for h, b in tiles:
        vb = blocks[b][1]
        acc[h] = acc[h] + (_dot_nt(w[h, b], vb) if keys_on_lanes else _dot(w[h, b], vb))
    return tuple((c_run[h], acc[h]) for h in range(len(qs)))


def _sb_prompt_kernel(bias_ref, q_ref, k_ref, v_ref, o_ref, kbf, vbf, *, seq, tq, nqb, nh):
    p = pl.program_id(1)
    qi = pl.program_id(2)
    pad = nqb * tq - seq
    hd = SB_HEAD_DIM
    width = nh * hd

    @pl.when(qi == 0)
    def _():
        kbf[0:seq, :] = k_ref[0].astype(BF16)
        vbf[0:seq, :] = v_ref[0].astype(BF16)
        if pad:
            kbf[seq:seq + pad, :] = jnp.zeros((pad, width), BF16)
            vbf[seq:seq + pad, :] = jnp.zeros((pad, width), BF16)

    def run(rows):
        q = q_ref[0, 0:rows, :] * (hd ** -0.5)
        head = lax.broadcasted_iota(jnp.int32, (rows, width), 1) // hd
        rr = lax.broadcasted_iota(jnp.int32, (rows, tq), 0)
        cc = lax.broadcasted_iota(jnp.int32, (rows, tq), 1)
        diag_mask = cc < rr
        neg_upper = _neg_strict_upper(tq)
        zero = jnp.zeros_like(q)
        q_heads = [jnp.where(head == h, q, zero) for h in range(nh)]
        biases = [bias_ref[nh * p + h] for h in range(nh)]

        def block(j, mask):
            off = pl.multiple_of(j * tq, tq)
            return kbf[pl.ds(off, tq), :], vbf[pl.ds(off, tq), :], mask

        def visit(carry, *blocks):
            return _sb_tiles(q_heads, biases, blocks, carry, neg_upper)

        c0 = jnp.zeros((rows, 1), F32)
        a0 = jnp.zeros((rows, width), F32)
        carry = visit(((c0, a0),) * nh, block(qi, diag_mask))
        carry = lax.cond(qi % 2 == 1, lambda cr: visit(cr, block(qi - 1, None)), lambda cr: cr, carry)
        top = qi - 1 - qi % 2
        carry = lax.fori_loop(
            0, qi // 2, lambda i, cr: visit(cr, block(top - 2 * i, None), block(top - 2 * i - 1, None)), carry)
        out = carry[0][1]
        for h in range(1, nh):
            out = jnp.where(head == h, carry[h][1], out)
        o_ref[0, 0:rows, :] = out.astype(o_ref.dtype)

    tail = seq - (nqb - 1) * tq
    if tail == tq:
        run(tq)
    else:
        pl.when(qi < nqb - 1)(lambda: run(tq))
        pl.when(qi == nqb - 1)(lambda: run(tail))


def sb_prompt_attention(q16, k32, v32, sb_bias, name):
    bsz, seq, d = q16.shape
    tq = SB_TQ
    nqb = pl.cdiv(seq, tq)
    assert (seq - (nqb - 1) * tq) % SUB_BF16 == 0
    nh = SB_GROUP
    w2 = nh * SB_HEAD_DIM
    kern = functools.partial(_sb_prompt_kernel, seq=seq, tq=tq, nqb=nqb, nh=nh)
    return pl.pallas_call(
        kern,
        grid_spec=pltpu.PrefetchScalarGridSpec(
            num_scalar_prefetch=1,
            grid=(bsz, d // w2, nqb),
            in_specs=[pl.BlockSpec((1, tq, w2), lambda b, p, i, s: (b, i, p)),
                      pl.BlockSpec((1, seq, w2), lambda b, p, i, s: (b, 0, p)),
                      pl.BlockSpec((1, seq, w2), lambda b, p, i, s: (b, 0, p))],
            out_specs=pl.BlockSpec((1, tq, w2), lambda b, p, i, s: (b, i, p)),
            scratch_shapes=[pltpu.VMEM((nqb * tq, w2), BF16), pltpu.VMEM((nqb * tq, w2), BF16)]),
        out_shape=jax.ShapeDtypeStruct((bsz, seq, d), BF16),
        compiler_params=_params(("arbitrary", "arbitrary", "arbitrary")),
        name=name,
    )(sb_bias, q16, k32, v32)


def _sb_sample_kernel(pt_ref, bias_ref, q_ref, kn_ref, vn_ref, *rest, pps, t_new, page):
    k_pages = rest[:pps]
    v_pages = rest[pps:2 * pps]
    o_ref = rest[2 * pps]
    qbd_ref, bias_mat, c_ref, acc_ref = rest[2 * pps + 1:]
    s = pl.program_id(1)
    hd = SB_HEAD_DIM
    d = SB_HEADS * hd
    rows = SB_HEADS * t_new
    neg_upper = _neg_strict_upper(page)

    def visit(blocks, keys_on_lanes):
        ((c, acc),) = _sb_tiles([qbd_ref[...]], [bias_mat[...]], blocks, [(c_ref[...], acc_ref[...])], neg_upper,
                                keys_on_lanes)
        c_ref[...] = c
        acc_ref[...] = acc

    @pl.when(s == 0)
    def _():
        q = q_ref[0] * (hd ** -0.5)
        qrep = jnp.concatenate([q] * SB_HEADS, axis=0)
        rh = lax.broadcasted_iota(jnp.int32, (rows, d), 0) // t_new
        ch = lax.broadcasted_iota(jnp.int32, (rows, d), 1) // hd
        qbd_ref[...] = jnp.where(rh == ch, qrep, 0.0).astype(BF16)
        rh2 = lax.broadcasted_iota(jnp.int32, (rows, page), 0) // t_new
        bm = jnp.zeros((rows, page), F32)
        for h in range(SB_HEADS):
            bm = jnp.where(rh2 == h, bias_ref[h], bm)
        bias_mat[...] = bm
        c_ref[...] = jnp.zeros_like(c_ref)
        acc_ref[...] = jnp.zeros_like(acc_ref)
        zpad = jnp.zeros((page - t_new, d), F32)
        kn = jnp.concatenate([kn_ref[0], zpad], axis=0).astype(BF16)
        vn = jnp.concatenate([vn_ref[0], zpad], axis=0).astype(BF16)
        tq = lax.broadcasted_iota(jnp.int32, (rows, page), 0) % t_new
        sk = lax.broadcasted_iota(jnp.int32, (rows, page), 1)
        visit([(kn, vn, sk < tq)], False)

    visit([(k_pages[r][0].astype(BF16), v_pages[r][0].astype(BF16), None) for r in range(pps)], True)

    @pl.when(s == pl.num_programs(1) - 1)
    def _():
        acc = acc_ref[...]
        ch = lax.broadcasted_iota(jnp.int32, (t_new, d), 1) // hd
        out = jnp.zeros((t_new, d), F32)
        for h in range(SB_HEADS):
            out = out + jnp.where(ch == h, acc[h * t_new:(h + 1) * t_new, :], 0.0)
        o_ref[0] = out


def sb_sample_attention(q, k_new, v_new, cache_k, cache_v, page_table, sb_bias, name):
    bsz, t_new, d = q.shape
    page = cache_k.shape[2]
    n_pages = page_table.shape[1]
    assert page == LANES and SB_HEADS * t_new == LANES
    pps = _pick(n_pages, (8, 4, 2, 1))
    nstep = n_pages // pps
    pt = page_table.reshape(-1).astype(jnp.int32)

    def page_map(r):
        return lambda b, s, pt_ref, bias_ref: (pt_ref[b * n_pages + (n_pages - 1 - (s * pps + r))], 0, 0)

    tok = pl.BlockSpec((1, t_new, d), lambda b, s, *_: (b, 0, 0))
    page_specs = [pl.BlockSpec((1, d, page), page_map(r)) for r in range(pps)]
    kern = functools.partial(_sb_sample_kernel, pps=pps, t_new=t_new, page=page)
    return pl.pallas_call(
        kern,
        grid_spec=pltpu.PrefetchScalarGridSpec(
            num_scalar_prefetch=2,
            grid=(bsz, nstep),
            in_specs=[tok, tok, tok] + page_specs + page_specs,
            out_specs=tok,
            scratch_shapes=[pltpu.VMEM((LANES, d), BF16), pltpu.VMEM((LANES, page), F32),
                            pltpu.VMEM((LANES, page), F32), pltpu.VMEM((LANES, d), F32)]),
        out_shape=jax.ShapeDtypeStruct((bsz, t_new, d), F32),
        compiler_params=_params(("arbitrary", "arbitrary")),
        name=name,
    )(pt, sb_bias, q, k_new, v_new, *([cache_k] * pps), *([cache_v] * pps))


def _gate_kernel(x_ref, wl_ref, w2_ref, bg_ref, o_ref):
    g_low = _dot(x_ref[...].astype(BF16), wl_ref[...].astype(BF16))
    a_hi, a_lo = _split2(g_low)
    w_hi, w_lo = _split2(w2_ref[...])
    pre = _dot(a_hi, w_hi) + _dot(a_hi, w_lo) + _dot(a_lo, w_hi) + bg_ref[...]
    o_ref[...] = -_softplus(-pre) * (1.0 / GLA_TAU)


def gla_log_gate(x, w_low, w_gate2, b_gate, name):
    m, k = x.shape
    rank, kd = w_gate2.shape
    wl = jnp.pad(w_low, ((0, 0), (0, LANES - rank)))
    w2 = jnp.pad(w_gate2, ((0, LANES - rank), (0, 0)))
    tm = _pick(m, (2064, 1024, 512, 256, 128, 16))
    row = lambda i: (i, 0)
    fix = lambda i: (0, 0)
    return pl.pallas_call(
        _gate_kernel,
        grid=(m // tm,),
        in_specs=[pl.BlockSpec((tm, k), row), pl.BlockSpec((k, LANES), fix), pl.BlockSpec((LANES, kd), fix),
                  pl.BlockSpec((1, kd), fix)],
        out_specs=pl.BlockSpec((tm, kd), row),
        out_shape=jax.ShapeDtypeStruct((m, kd), F32),
        compiler_params=_params(("arbitrary",)),
        name=name,
    )(x, wl, w2, b_gate.reshape(1, kd))


def _gla_chunks(qs, ks, vs, lgs, states):
    heads = range(len(qs))
    c, dk = qs[0].shape
    dv = vs[0].shape[1]
    sub = GLA_SUB
    nsub = c // sub
    ri = lax.broadcasted_iota(jnp.int32, (c, c), 0)
    ci = lax.broadcasted_iota(jnp.int32, (c, c), 1)
    lower = jnp.where(ri >= ci, 1.0, 0.0).astype(BF16)
    ones_c = jnp.ones((c, dk), BF16)
    ones_k = jnp.ones((dk, LANES), BF16)
    row16 = lax.broadcasted_iota(jnp.int32, (sub, dk), 0)
    lane16 = lax.broadcasted_iota(jnp.int32, (sub, LANES), 1)
    rowc = lax.broadcasted_iota(jnp.int32, (c, dk), 0)
    zrows = jnp.zeros((LANES - c, dk), BF16) if c < LANES else None

    parts = [_split3(lgs[h]) for h in heads]
    b = [_dot(lower, parts[h][0]) + _dot(lower, parts[h][1]) + _dot(lower, parts[h][2]) for h in heads]
    b_end_col = [_dot_tn(parts[h][0], ones_c) + _dot_tn(parts[h][1], ones_c) + _dot_tn(parts[h][2], ones_c)
                 for h in heads]

    q_in = [(qs[h] * jnp.exp(b[h])).astype(BF16) for h in heads]
    k_end = [(ks[h] * jnp.exp(b[h][c - 1:c, :] - b[h])).astype(BF16) for h in heads]
    o = [_dot(q_in[h], states[h].astype(BF16)) for h in heads]
    kv = [_dot_tn(k_end[h], vs[h]) for h in heads]
    new_states = [jnp.concatenate([jnp.exp(b_end_col[h])] * (dv // dk), axis=1) * states[h] + kv[h] for h in heads]

    pieces = []
    for h in heads:
        ph = []
        for i in range(nsub):
            qsub = qs[h][i * sub:(i + 1) * sub]
            ksub = ks[h][i * sub:(i + 1) * sub]
            bsub = b[h][i * sub:(i + 1) * sub]
            for s in range(sub):
                e = jnp.exp(jnp.where(row16 >= s, bsub - bsub[s:s + 1, :], -1e30))
                ph.append((qsub * e * ksub[s:s + 1, :]).astype(BF16))
        pieces.append(jnp.concatenate(ph, axis=0))
    pair = [_dot(pieces[h], ones_k) for h in heads]
    qq, kk = {}, {}
    for h in heads:
        for i in range(1, nsub):
            b_ref = b[h][i * sub - 1:i * sub, :]
            qq[h, i] = (qs[h][i * sub:(i + 1) * sub] * jnp.exp(b[h][i * sub:(i + 1) * sub] - b_ref)).astype(BF16)
            kki = jnp.where(rowc < i * sub, ks[h] * jnp.exp(jnp.minimum(b_ref - b[h], 0.0)), 0.0).astype(BF16)
            kk[h, i] = kki if zrows is None else jnp.concatenate([kki, zrows], axis=0)
    cross = {key: _dot_nt(qq[key], kk[key]) for key in qq}
    scores = []
    for h in heads:
        rows = []
        for i in range(nsub):
            sc = jnp.zeros((sub, LANES), F32)
            for s in range(sub):
                blk = pair[h][(i * sub + s) * sub:(i * sub + s + 1) * sub]
                sc = jnp.where(lane16 == i * sub + s, blk, sc)
            if i > 0:
                sc = sc + cross[h, i]
            rows.append(sc)
        scores.append(jnp.concatenate(rows, axis=0).astype(BF16))
    vpad = [vs[h] if c == LANES else jnp.concatenate([vs[h], jnp.zeros((LANES - c, dv), BF16)], axis=0)
            for h in heads]
    o = [o[h] + _dot(scores[h], vpad[h]) for h in heads]
    return o, new_states


def _gla_finish(o, r, gn):
    ms = jnp.mean(o * o, axis=-1, keepdims=True)
    return o * lax.rsqrt(ms + RMS_EPS) * gn * (r * _sigmoid(r))


def _gla_prompt_kernel(q_ref, k_ref, lg_ref, v_ref, r_ref, gn_ref, og_ref, s_ref, state):
    scale = GLA_HEAD_K ** -0.5
    dk, dv, nh = GLA_HEAD_K, GLA_HEAD_V, GLA_HEADS
    ci = pl.program_id(1)

    @pl.when(ci == 0)
    def _():
        state[...] = jnp.zeros_like(state)

    kslice = [slice(h * dk, (h + 1) * dk) for h in range(nh)]
    vslice = [slice(h * dv, (h + 1) * dv) for h in range(nh)]
    o, s_new = _gla_chunks([q_ref[0, :, ks] * scale for ks in kslice], [k_ref[0, :, ks] for ks in kslice],
                           [v_ref[0, :, vs] for vs in vslice], [lg_ref[0, :, ks] for ks in kslice],
                           [state[h] for h in range(nh)])
    gn = gn_ref[...]
    for h in range(nh):
        state[h] = s_new[h]
        og_ref[0, :, vslice[h]] = _gla_finish(o[h], r_ref[0, :, vslice[h]].astype(F32), gn).astype(og_ref.dtype)

    @pl.when(ci == pl.num_programs(1) - 1)
    def _():
        s_ref[0] = state[...]


def gla_prompt(q, k, lg, v16, r16, gnorm_g, name):
    bsz, seq, _ = q.shape
    dk, dv, nh = GLA_HEAD_K, GLA_HEAD_V, GLA_HEADS
    chunk = _pick(seq, GLA_CHUNKS)
    kspec = pl.BlockSpec((1, chunk, nh * dk), lambda b, c: (b, c, 0))
    vspec = pl.BlockSpec((1, chunk, nh * dv), lambda b, c: (b, c, 0))
    return pl.pallas_call(
        _gla_prompt_kernel,
        grid=(bsz, seq // chunk),
        in_specs=[kspec, kspec, kspec, vspec, vspec, pl.BlockSpec((1, dv), lambda b, c: (0, 0))],
        out_specs=[vspec, pl.BlockSpec((1, nh, dk, dv), lambda b, c: (b, 0, 0, 0))],
        out_shape=[jax.ShapeDtypeStruct((bsz, seq, nh * dv), BF16),
                   jax.ShapeDtypeStruct((bsz, nh, dk, dv), F32)],
        scratch_shapes=[pltpu.VMEM((nh, dk, dv), F32)],
        compiler_params=_params(("arbitrary", "arbitrary")),
        name=name,
    )(q, k, lg, v16, r16, gnorm_g.reshape(1, dv))


def _gla_sample_kernel(q_ref, k_ref, lg_ref, v_ref, r_ref, gn_ref, s0_ref, og_ref, s_ref, *, t_new):
    scale = GLA_HEAD_K ** -0.5
    dk, dv, nh = GLA_HEAD_K, GLA_HEAD_V, GLA_HEADS
    padk = jnp.zeros((GLA_SUB - t_new, dk), F32)
    padv = jnp.zeros((GLA_SUB - t_new, dv), F32)
    kslice = [slice(h * dk, (h + 1) * dk) for h in range(nh)]
    vslice = [slice(h * dv, (h + 1) * dv) for h in range(nh)]
    o, s_new = _gla_chunks(
        [jnp.concatenate([q_ref[0, :, ks] * scale, padk], axis=0) for ks in kslice],
        [jnp.concatenate([k_ref[0, :, ks], padk], axis=0) for ks in kslice],
        [jnp.concatenate([v_ref[0, :, vs], padv], axis=0).astype(BF16) for vs in vslice],
        [jnp.concatenate([lg_ref[0, :, ks], padk], axis=0) for ks in kslice],
        [s0_ref[0, h] for h in range(nh)])
    gn = gn_ref[...]
    for h in range(nh):
        s_ref[0, h] = s_new[h]
        og_ref[0, :, vslice[h]] = _gla_finish(o[h][0:t_new], r_ref[0, :, vslice[h]], gn)


def gla_sample(q, k, lg, v, r, gnorm_g, state, name):
    bsz, t_new, _ = q.shape
    dk, dv, nh = GLA_HEAD_K, GLA_HEAD_V, GLA_HEADS
    assert t_new <= GLA_SUB
    kspec = pl.BlockSpec((1, t_new, nh * dk), lambda b: (b, 0, 0))
    vspec = pl.BlockSpec((1, t_new, nh * dv), lambda b: (b, 0, 0))
    sspec = pl.BlockSpec((1, nh, dk, dv), lambda b: (b, 0, 0, 0))
    kern = functools.partial(_gla_sample_kernel, t_new=t_new)
    return pl.pallas_call(
        kern,
        grid=(bsz,),
        in_specs=[kspec, kspec, kspec, vspec, vspec, pl.BlockSpec((1, dv), lambda b: (0, 0)), sspec],
        out_specs=[vspec, sspec],
        out_shape=[jax.ShapeDtypeStruct((bsz, t_new, nh * dv), F32),
                   jax.ShapeDtypeStruct((bsz, nh, dk, dv), F32)],
        compiler_params=_params(("arbitrary",)),
        name=name,
    )(q, k, lg, v, r, gnorm_g.reshape(1, dv), state)


def _router_kernel(h_ref, wr_ref, meta_ref, cnt_ref, carry, *, n_tok, tm):
    i = pl.program_id(0)

    @pl.when(i == 0)
    def _():
        carry[...] = jnp.zeros_like(carry)

    hrow = lax.broadcasted_iota(jnp.int32, h_ref.shape, 0)
    h_hi, h_lo = _split2(jnp.where(i * tm + hrow < n_tok, h_ref[...], 0.0))
    w_hi, w_lo = _split2(wr_ref[...])
    logits = _dot(h_hi, w_hi) + _dot(h_lo, w_hi) + _dot(h_hi, w_lo)
    lane = lax.broadcasted_iota(jnp.int32, (tm, LANES), 1)
    row = lax.broadcasted_iota(jnp.int32, (tm, LANES), 0)
    neg = -jnp.inf
    l1 = jnp.where(lane < N_EXPERTS, logits, neg)
    m1 = jnp.max(l1, axis=1, keepdims=True)
    i1 = jnp.min(jnp.where(l1 == m1, lane, LANES), axis=1, keepdims=True)
    l2 = jnp.where(lane == i1, neg, l1)
    m2 = jnp.max(l2, axis=1, keepdims=True)
    i2 = jnp.min(jnp.where(l2 == m2, lane, LANES), axis=1, keepdims=True)
    e = jnp.exp(m2 - m1)
    w1 = 1.0 / (1.0 + e)
    w2 = e * w1
    valid = (i * tm + row) < n_tok
    hit = jnp.where(valid & ((lane == i1) | (lane == i2)), 1.0, 0.0)
    ri = lax.broadcasted_iota(jnp.int32, (tm, tm), 0)
    ci = lax.broadcasted_iota(jnp.int32, (tm, tm), 1)
    before = jnp.where(ri > ci, 1.0, 0.0).astype(BF16)
    rank = _dot(before, hit.astype(BF16)) + carry[0:1, :]
    r1 = jnp.sum(jnp.where(lane == i1, rank, 0.0), axis=1, keepdims=True)
    r2 = jnp.sum(jnp.where(lane == i2, rank, 0.0), axis=1, keepdims=True)
    carry[0:1, :] = carry[0:1, :] + jnp.sum(hit, axis=0, keepdims=True)
    meta = jnp.where(lane == 0, i1.astype(F32), 0.0)
    meta = jnp.where(lane == 1, i2.astype(F32), meta)
    meta = jnp.where(lane == 2, w1, meta)
    meta = jnp.where(lane == 3, w2, meta)
    meta = jnp.where(lane == 4, r1, meta)
    meta = jnp.where(lane == 5, r2, meta)
    meta_ref[...] = meta

    @pl.when(i == pl.num_programs(0) - 1)
    def _():
        cnt_ref[...] = carry[...]


def moe_route(h32, w_router):
    n_tok, d = h32.shape
    tm = ROUTE_TM
    nt = pl.cdiv(n_tok, tm)
    wr = jnp.pad(w_router, ((0, 0), (0, LANES - w_router.shape[1])))
    kern = functools.partial(_router_kernel, n_tok=n_tok, tm=tm)
    return pl.pallas_call(
        kern,
        grid=(nt,),
        in_specs=[pl.BlockSpec((tm, d), lambda i: (i, 0)), pl.BlockSpec((d, LANES), lambda i: (0, 0))],
        out_specs=[pl.BlockSpec((tm, LANES), lambda i: (i, 0)), pl.BlockSpec((8, LANES), lambda i: (0, 0))],
        out_shape=[jax.ShapeDtypeStruct((nt * tm, LANES), F32), jax.ShapeDtypeStruct((8, LANES), F32)],
        scratch_shapes=[pltpu.VMEM((8, LANES), F32)],
        compiler_params=_params(("arbitrary",)),
        name="moe_route",
    )(h32, wr)


def _row_copy(src, dst, i, j, sem):
    return pltpu.make_async_copy(src.at[pl.ds(i, 1)], dst.at[pl.ds(j, 1)], sem)


def _dispatch_kernel(pos_ref, h_ref, xs_in, xs_out, sem, *, tm):
    del xs_in

    def for_each(fn):
        def body(r, carry):
            for slot in range(2):
                fn(_row_copy(h_ref, xs_out, r, pos_ref[0, 0, 2 * r + slot], sem))
            return carry
        lax.fori_loop(0, tm, body, 0, unroll=DMA_UNROLL)

    for_each(lambda cp: cp.start())
    for_each(lambda cp: cp.wait())


def moe_dispatch(h32, pos, n_rows):
    n_tok, d = h32.shape
    tm = _row_tile(n_tok, 1536)
    nt = n_tok // tm
    pos = pos.reshape(nt, 1, 2 * tm)
    kern = functools.partial(_dispatch_kernel, tm=tm)
    any_spec = pl.BlockSpec(memory_space=pl.ANY)
    return pl.pallas_call(
        kern,
        grid=(nt,),
        in_specs=[pl.BlockSpec((1, 1, 2 * tm), lambda i: (i, 0, 0), memory_space=pltpu.SMEM),
                  pl.BlockSpec((tm, d), lambda i: (i, 0)), any_spec],
        out_specs=any_spec,
        out_shape=jax.ShapeDtypeStruct((n_rows, d), F32),
        scratch_shapes=[pltpu.SemaphoreType.DMA(())],
        input_output_aliases={2: 0},
        compiler_params=_params(("arbitrary",)),
        name="moe_dispatch",
    )(pos, h32, jnp.zeros((n_rows, d), F32))


def _experts_kernel(te_ref, tv_ref, x_ref, wg_ref, wu_ref, wd_ref, y_ref, xbf, acc):
    t = pl.program_id(0)
    f = pl.program_id(1)
    live = tv_ref[t] > 0

    @pl.when(f == 0)
    def _():
        xbf[...] = x_ref[...].astype(BF16)
        acc[...] = jnp.zeros_like(acc)

    @pl.when(live)
    def _():
        x = xbf[...]
        gate = _dot(x, wg_ref[0].astype(BF16))
        up = _dot(x, wu_ref[0].astype(BF16))
        act = (gate * _sigmoid(gate) * up).astype(BF16)
        acc[...] += _dot(act, wd_ref[0].astype(BF16))

    @pl.when(f == pl.num_programs(1) - 1)
    def _():
        y_ref[...] = acc[...]


def moe_experts(xs, w_gu_e, w_down_e, tile_expert, tile_live):
    n_rows, d = xs.shape
    dff = w_down_e.shape[1]
    tm, tf = MOE_TM, MOE_TF
    nf = dff // tf
    assert nf * tf == dff and n_rows % tm == 0

    def fcol(t, f, tv):
        return jnp.where(tv[t] > 0, f, nf - 1)

    return pl.pallas_call(
        _experts_kernel,
        grid_spec=pltpu.PrefetchScalarGridSpec(
            num_scalar_prefetch=2,
            grid=(n_rows // tm, nf),
            in_specs=[pl.BlockSpec((tm, d), lambda t, f, te, tv: (t, 0)),
                      pl.BlockSpec((1, d, tf), lambda t, f, te, tv: (te[t], 0, fcol(t, f, tv))),
                      pl.BlockSpec((1, d, tf), lambda t, f, te, tv: (te[t], 0, nf + fcol(t, f, tv))),
                      pl.BlockSpec((1, tf, d), lambda t, f, te, tv: (te[t], fcol(t, f, tv), 0))],
            out_specs=pl.BlockSpec((tm, d), lambda t, f, te, tv: (t, 0)),
            scratch_shapes=[pltpu.VMEM((tm, d), BF16), pltpu.VMEM((tm, d), F32)]),
        out_shape=jax.ShapeDtypeStruct((n_rows, d), F32),
        compiler_params=_params(("arbitrary", "arbitrary")),
        name="moe_experts",
    )(tile_expert, tile_live, xs, w_gu_e, w_gu_e, w_down_e)


def _combine_kernel(pos_ref, meta_ref, res_ref, g_ref, b_ref, y_hbm, o32_ref, o16_ref, y1, y2, sem, *, tm):
    def for_each(fn):
        def body(r, carry):
            fn(_row_copy(y_hbm, y1, pos_ref[0, 0, 2 * r], r, sem))
            fn(_row_copy(y_hbm, y2, pos_ref[0, 0, 2 * r + 1], r, sem))
            return carry
        lax.fori_loop(0, tm, body, 0, unroll=DMA_UNROLL)

    for_each(lambda cp: cp.start())
    for_each(lambda cp: cp.wait())
    meta = meta_ref[...]
    mix = meta[:, 2:3] * y1[...] + meta[:, 3:4] * y2[...]
    h = _layer_norm(DEEPNORM_ALPHA * res_ref[...] + mix, g_ref[...], b_ref[...])
    o32_ref[...] = h
    o16_ref[...] = h.astype(BF16)


def moe_combine_deepnorm(ys, pos, meta, res, g, b, name):
    m, d = res.shape
    tm = _pick(m, (688, 512, 256, 128, 16))
    nt = m // tm
    row = lambda i: (i, 0)
    fix = lambda i: (0, 0)
    kern = functools.partial(_combine_kernel, tm=tm)
    return pl.pallas_call(
        kern,
        grid=(nt,),
        in_specs=[pl.BlockSpec((1, 1, 2 * tm), lambda i: (i, 0, 0), memory_space=pltpu.SMEM),
                  pl.BlockSpec((tm, LANES), row), pl.BlockSpec((tm, d), row),
                  pl.BlockSpec((1, d), fix), pl.BlockSpec((1, d), fix), pl.BlockSpec(memory_space=pl.ANY)],
        out_specs=[pl.BlockSpec((tm, d), row), pl.BlockSpec((tm, d), row)],
        out_shape=[jax.ShapeDtypeStruct((m, d), F32), jax.ShapeDtypeStruct((m, d), BF16)],
        scratch_shapes=[pltpu.VMEM((tm, d), F32), pltpu.VMEM((tm, d), F32), pltpu.SemaphoreType.DMA(())],
        compiler_params=_params(("arbitrary",)),
        name=name,
    )(pos.reshape(nt, 1, 2 * tm), meta, res, g.reshape(1, d), b.reshape(1, d), ys)


def moe_deepnorm(hp32, hs32, w_router, w_gu_e, w_down_e, g, b, layer):
    mp = hp32.shape[0]
    h_all = jnp.concatenate([hp32, hs32], axis=0)
    n_tok = h_all.shape[0]
    meta, cnt = moe_route(h_all, w_router)
    meta = meta[:n_tok]
    counts = cnt[0, :N_EXPERTS].astype(jnp.int32)
    padded = ((counts + MOE_TM - 1) // MOE_TM) * MOE_TM
    ends = jnp.cumsum(padded)
    starts = ends - padded
    e1 = meta[:, 0].astype(jnp.int32)
    e2 = meta[:, 1].astype(jnp.int32)
    eids = jnp.arange(N_EXPERTS, dtype=jnp.int32)
    start1 = jnp.sum(jnp.where(e1[:, None] == eids[None, :], starts[None, :], 0), axis=1)
    start2 = jnp.sum(jnp.where(e2[:, None] == eids[None, :], starts[None, :], 0), axis=1)
    pos = jnp.stack([start1 + meta[:, 4].astype(jnp.int32), start2 + meta[:, 5].astype(jnp.int32)], axis=1)
    n_tiles = pl.cdiv(2 * n_tok, MOE_TM) + N_EXPERTS
    tile_start = jnp.arange(n_tiles, dtype=jnp.int32) * MOE_TM
    tile_live = (tile_start < ends[-1]).astype(jnp.int32)
    tile_expert = jnp.sum((tile_start[:, None] >= ends[None, :]).astype(jnp.int32), axis=1)
    last_live = jnp.sum((jnp.maximum(ends[-1] - 1, 0) >= ends).astype(jnp.int32))
    tile_expert = jnp.where(tile_live > 0, tile_expert, last_live).astype(jnp.int32)

    xs = moe_dispatch(h_all, pos, n_tiles * MOE_TM)
    ys = moe_experts(xs, w_gu_e, w_down_e, tile_expert, tile_live)
    outp = moe_combine_deepnorm(ys, pos[:mp], meta[:mp], hp32, g, b, f"l{layer}_moe_combine_p")
    outs = moe_combine_deepnorm(ys, pos[mp:], meta[mp:], hs32, g, b, f"l{layer}_moe_combine_s")
    return outp, outs


def _sb_layer(i, hp32, hp16, hs32, hs16, bsz, seq, dec_b, dec_t, cache_k, cache_v, page_table,
              w_qkv, w_o, sb_bias, ln1_g, ln1_b):
    d = D_MODEL
    tag = f"l{i}_"
    qp = linear(hp16, w_qkv, 0, d, BF16, tag + "q_p")
    kp = linear(hp16, w_qkv, d, d, F32, tag + "k_p")
    vp = linear(hp16, w_qkv, 2 * d, d, F32, tag + "v_p")
    qs = linear(hs16, w_qkv, 0, d, F32, tag + "q_s")
    ks = linear(hs16, w_qkv, d, d, F32, tag + "k_s")
    vs = linear(hs16, w_qkv, 2 * d, d, F32, tag + "v_s")
    op = sb_prompt_attention(qp.reshape(bsz, seq, d), kp.reshape(bsz, seq, d), vp.reshape(bsz, seq, d),
                             sb_bias, tag + "sb_p")
    pool, page = cache_k.shape[0], cache_k.shape[1]
    ck = jnp.transpose(cache_k, (0, 2, 3, 1)).reshape(pool, d, page)
    cv = jnp.transpose(cache_v, (0, 2, 3, 1)).reshape(pool, d, page)
    os_ = sb_sample_attention(qs.reshape(dec_b, dec_t, d), ks.reshape(dec_b, dec_t, d), vs.reshape(dec_b, dec_t, d),
                              ck, cv, page_table, sb_bias, tag + "sb_s")
    hp32, hp16 = linear_deepnorm(op.reshape(bsz * seq, d), w_o, hp32, ln1_g, ln1_b, tag + "o_p")
    hs32, hs16 = linear_deepnorm(os_.reshape(dec_b * dec_t, d), w_o, hs32, ln1_g, ln1_b, tag + "o_s")
    shp = (bsz, seq, SB_HEADS, SB_HEAD_DIM)
    shs = (dec_b, dec_t, SB_HEADS, SB_HEAD_DIM)
    new_state = [kp.reshape(shp), vp.reshape(shp), ks.reshape(shs), vs.reshape(shs)]
    return hp32, hp16, hs32, hs16, new_state


def _gla_layer(i, hp32, hp16, hs32, hs16, bsz, seq, dec_b, dec_t, state,
               w_in, w_gate2, b_gate, gnorm_g, w_o, ln1_g, ln1_b):
    tag = f"l{i}_"
    kd, vd = GLA_KEY_DIM, GLA_HEADS * GLA_HEAD_V
    w_low = w_in[:, 2 * kd + 2 * vd:]

    def project(x16, vdtype, sfx):
        q = linear(x16, w_in, 0, kd, F32, tag + "q" + sfx)
        k = linear(x16, w_in, kd, kd, F32, tag + "k" + sfx)
        v = linear(x16, w_in, 2 * kd, vd, vdtype, tag + "v" + sfx)
        r = linear(x16, w_in, 2 * kd + vd, vd, vdtype, tag + "r" + sfx)
        lg = gla_log_gate(x16, w_low, w_gate2, b_gate, tag + "gate" + sfx)
        return q, k, lg, v, r

    q, k, lg, v, r = project(hp16, BF16, "_p")
    og_p, sp = gla_prompt(q.reshape(bsz, seq, kd), k.reshape(bsz, seq, kd), lg.reshape(bsz, seq, kd),
                          v.reshape(bsz, seq, vd), r.reshape(bsz, seq, vd), gnorm_g, tag + "gla_p")
    q, k, lg, v, r = project(hs16, F32, "_s")
    og_s, ss = gla_sample(q.reshape(dec_b, dec_t, kd), k.reshape(dec_b, dec_t, kd), lg.reshape(dec_b, dec_t, kd),
                          v.reshape(dec_b, dec_t, vd), r.reshape(dec_b, dec_t, vd), gnorm_g, state, tag + "gla_s")
    hp32, hp16 = linear_deepnorm(og_p.reshape(bsz * seq, vd), w_o, hp32, ln1_g, ln1_b, tag + "o_p")
    hs32, hs16 = linear_deepnorm(og_s.reshape(dec_b * dec_t, vd), w_o, hs32, ln1_g, ln1_b, tag + "o_s")
    return hp32, hp16, hs32, hs16, [sp, ss]


def kernel(x_prompt, x_sample, cache_k0, cache_v0, state_s1, cache_k2, cache_v2, state_s3, page_table, meta_tokens, l0_w_qkv, l0_w_o, l0_sb_bias, l0_ln1_g, l0_ln1_b, l0_w_gu, l0_w_down, l0_ln2_g, l0_ln2_b, l1_w_in, l1_w_gate2, l1_b_gate, l1_gnorm_g, l1_w_o, l1_ln1_g, l1_ln1_b, l1_w_router, l1_w_gu_e, l1_w_down_e, l1_ln2_g, l1_ln2_b, l2_w_qkv, l2_w_o, l2_sb_bias, l2_ln1_g, l2_ln1_b, l2_w_gu, l2_w_down, l2_ln2_g, l2_ln2_b, l3_w_in, l3_w_gate2, l3_b_gate, l3_gnorm_g, l3_w_o, l3_ln1_g, l3_ln1_b, l3_w_router, l3_w_gu_e, l3_w_down_e, l3_ln2_g, l3_ln2_b):
    bsz, seq0, d = x_prompt.shape
    dec_b, dec_t, _ = x_sample.shape
    n_meta = meta_tokens.shape[0]
    seq = n_meta + seq0
    meta = jnp.broadcast_to(meta_tokens.astype(x_prompt.dtype)[None], (bsz, n_meta, d))
    hp32 = jnp.concatenate([meta, x_prompt], axis=1).reshape(bsz * seq, d)
    hs32 = x_sample.reshape(dec_b * dec_t, d)
    hp16 = hp32.astype(BF16)
    hs16 = hs32.astype(BF16)

    sb = [(cache_k0, cache_v0, l0_w_qkv, l0_w_o, l0_sb_bias, l0_ln1_g, l0_ln1_b),
          (cache_k2, cache_v2, l2_w_qkv, l2_w_o, l2_sb_bias, l2_ln1_g, l2_ln1_b)]
    ffn = [(l0_w_gu, l0_w_down, l0_ln2_g, l0_ln2_b), (l2_w_gu, l2_w_down, l2_ln2_g, l2_ln2_b)]
    gla = [(state_s1, l1_w_in, l1_w_gate2, l1_b_gate, l1_gnorm_g, l1_w_o, l1_ln1_g, l1_ln1_b),
           (state_s3, l3_w_in, l3_w_gate2, l3_b_gate, l3_gnorm_g, l3_w_o, l3_ln1_g, l3_ln1_b)]
    moe = [(l1_w_router, l1_w_gu_e, l1_w_down_e, l1_ln2_g, l1_ln2_b),
           (l3_w_router, l3_w_gu_e, l3_w_down_e, l3_ln2_g, l3_ln2_b)]

    new_state = []
    for i in range(DEPTH):
        if i % 2 == 0:
            cache_k, cache_v, w_qkv, w_o, sb_bias, g1, b1 = sb[i // 2]
            hp32, hp16, hs32, hs16, st = _sb_layer(i, hp32, hp16, hs32, hs16, bsz, seq, dec_b, dec_t,
                                                   cache_k, cache_v, page_table, w_qkv, w_o, sb_bias, g1, b1)
            w_gu, w_down, g2, b2 = ffn[i // 2]
            hp32, hp16 = swiglu_deepnorm(hp16, w_gu, w_down, hp32, g2, b2, f"l{i}_ffn_p")
            hs32, hs16 = swiglu_deepnorm(hs16, w_gu, w_down, hs32, g2, b2, f"l{i}_ffn_s")
        else:
            state, w_in, w_gate2, b_gate, gnorm_g, w_o, g1, b1 = gla[i // 2]
            hp32, hp16, hs32, hs16, st = _gla_layer(i, hp32, hp16, hs32, hs16, bsz, seq, dec_b, dec_t, state,
                                                    w_in, w_gate2, b_gate, gnorm_g, w_o, g1, b1)
            w_router, w_gu_e, w_down_e, g2, b2 = moe[i // 2]
            (hp32, hp16), (hs32, hs16) = moe_deepnorm(hp32, hs32, w_router, w_gu_e, w_down_e, g2, b2, i)
        new_state += st

    y_prompt = hp32.reshape(bsz, seq, d)[:, n_meta:]
    y_sample = hs32.reshape(dec_b, dec_t, d)
    return (y_prompt, y_sample, *new_state)
```

```python
import functools

import jax
import jax.numpy as jnp
from jax import lax
from jax.experimental import pallas as pl
from jax.experimental.pallas import tpu as pltpu

F32 = jnp.float32
BF16 = jnp.bfloat16

D_MODEL = 1024
N_META = 16
SB_HEADS = 16
SB_HEAD_DIM = 64
GLA_HEADS = 4
GLA_HEAD_K = 128
GLA_HEAD_V = 256
GLA_KEY_DIM = GLA_HEADS * GLA_HEAD_K
GLA_GATE_RANK = 16
GLA_TAU = 16.0
N_EXPERTS = 8
DEPTH = 4
DEEPNORM_ALPHA = float((2 * DEPTH) ** 0.25)
LN_EPS = 1e-5
RMS_EPS = 1e-6

LANES = 128
SUB_BF16 = 16
VMEM_LIMIT = 56 * 1024 * 1024

SB_TQ = 256
SB_GROUP = 4
GLA_SUB = 16
GLA_CHUNKS = (48, 64, 32, 16)
MOE_TM = 1024
MOE_TF = 512
ROUTE_TM = 1024
DMA_UNROLL = 4
FFN_TF = 256


def _params(sem):
    return pltpu.CompilerParams(dimension_semantics=sem, vmem_limit_bytes=VMEM_LIMIT)


def _row_tile(m, cap):
    for t in range(cap - cap % 8, 0, -8):
        if m % t == 0:
            return t
    raise ValueError(f"no row tile for {m}")


def _pick(m, cands):
    for c in cands:
        if m % c == 0:
            return c
    raise ValueError(f"no tile for {m} in {cands}")


def _dot(a, b):
    return jnp.dot(a, b, preferred_element_type=F32)


def _dot_nt(a, b):
    return lax.dot_general(a, b, (((1,), (1,)), ((), ())), preferred_element_type=F32)


def _dot_tn(a, b):
    return lax.dot_general(a, b, (((0,), (0,)), ((), ())), preferred_element_type=F32)


def _softplus(z):
    return jnp.maximum(z, 0.0) + jnp.log1p(jnp.exp(-jnp.abs(z)))


def _sigmoid(x):
    return 1.0 / (1.0 + jnp.exp(-x))


def _split2(x):
    hi = x.astype(BF16)
    lo = (x - hi.astype(F32)).astype(BF16)
    return hi, lo


def _split3(x):
    a = x.astype(BF16)
    r = x - a.astype(F32)
    b = r.astype(BF16)
    c = (r - b.astype(F32)).astype(BF16)
    return a, b, c


def _layer_norm(y, g, b):
    mu = jnp.mean(y, axis=-1, keepdims=True)
    yc = y - mu
    var = jnp.mean(yc * yc, axis=-1, keepdims=True)
    return yc * lax.rsqrt(var + LN_EPS) * g + b


def _linear_kernel(x_ref, w_ref, o_ref, wbf_ref):
    @pl.when(pl.program_id(1) == 0)
    def _():
        wbf_ref[...] = w_ref[...].astype(BF16)

    o_ref[...] = _dot(x_ref[...].astype(BF16), wbf_ref[...]).astype(o_ref.dtype)


def linear(x, w, col0, ncols, out_dtype, name):
    m, k = x.shape
    tm = _pick(m, (2064, 1024, 512, 256, 128, 16))
    tn = _pick(ncols, (512, 256, 128))
    assert col0 % tn == 0
    return pl.pallas_call(
        _linear_kernel,
        grid=(ncols // tn, m // tm),
        in_specs=[pl.BlockSpec((tm, k), lambda j, i: (i, 0)),
                  pl.BlockSpec((k, tn), lambda j, i: (0, col0 // tn + j))],
        out_specs=pl.BlockSpec((tm, tn), lambda j, i: (i, j)),
        out_shape=jax.ShapeDtypeStruct((m, ncols), out_dtype),
        scratch_shapes=[pltpu.VMEM((k, tn), BF16)],
        compiler_params=_params(("arbitrary", "arbitrary")),
        name=name,
    )(x, w)


def _linear_ln_kernel(x_ref, w_ref, res_ref, g_ref, b_ref, o32_ref, o16_ref, wbf_ref):
    @pl.when(pl.program_id(0) == 0)
    def _():
        wbf_ref[...] = w_ref[...].astype(BF16)

    y = _dot(x_ref[...].astype(BF16), wbf_ref[...])
    h = _layer_norm(DEEPNORM_ALPHA * res_ref[...] + y, g_ref[...], b_ref[...])
    o32_ref[...] = h
    o16_ref[...] = h.astype(BF16)


def linear_deepnorm(x, w, res, g, b, name):
    m, k = x.shape
    d = w.shape[1]
    tm = _pick(m, (688, 512, 256, 128, 16))
    row = lambda i: (i, 0)
    fix = lambda i: (0, 0)
    return pl.pallas_call(
        _linear_ln_kernel,
        grid=(m // tm,),
        in_specs=[pl.BlockSpec((tm, k), row), pl.BlockSpec((k, d), fix), pl.BlockSpec((tm, d), row),
                  pl.BlockSpec((1, d), fix), pl.BlockSpec((1, d), fix)],
        out_specs=[pl.BlockSpec((tm, d), row), pl.BlockSpec((tm, d), row)],
        out_shape=[jax.ShapeDtypeStruct((m, d), F32), jax.ShapeDtypeStruct((m, d), BF16)],
        scratch_shapes=[pltpu.VMEM((k, d), BF16)],
        compiler_params=_params(("arbitrary",)),
        name=name,
    )(x, w, res, g.reshape(1, d), b.reshape(1, d))


def _swiglu_ln_kernel(x_ref, wg_ref, wu_ref, wd_ref, res_ref, g_ref, b_ref, o32_ref, o16_ref, acc_ref):
    f = pl.program_id(1)

    @pl.when(f == 0)
    def _():
        acc_ref[...] = jnp.zeros_like(acc_ref)

    x = x_ref[...]
    gate = _dot(x, wg_ref[...].astype(BF16))
    up = _dot(x, wu_ref[...].astype(BF16))
    act = (gate * _sigmoid(gate) * up).astype(BF16)
    acc_ref[...] += _dot(act, wd_ref[...].astype(BF16))

    @pl.when(f == pl.num_programs(1) - 1)
    def _():
        h = _layer_norm(DEEPNORM_ALPHA * res_ref[...] + acc_ref[...], g_ref[...], b_ref[...])
        o32_ref[...] = h
        o16_ref[...] = h.astype(BF16)


def swiglu_deepnorm(x16, w_gu, w_down, res, g, b, name):
    m, d = x16.shape
    dff = w_down.shape[0]
    tm = _pick(m, (1376, 512, 256, 128, 16))
    tf = FFN_TF
    nf = dff // tf
    assert nf * tf == dff
    row = lambda i, f: (i, 0)
    fix = lambda i, f: (0, 0)
    return pl.pallas_call(
        _swiglu_ln_kernel,
        grid=(m // tm, nf),
        in_specs=[pl.BlockSpec((tm, d), row),
                  pl.BlockSpec((d, tf), lambda i, f: (0, f)),
                  pl.BlockSpec((d, tf), lambda i, f: (0, nf + f)),
                  pl.BlockSpec((tf, d), lambda i, f: (f, 0)),
                  pl.BlockSpec((tm, d), row), pl.BlockSpec((1, d), fix), pl.BlockSpec((1, d), fix)],
        out_specs=[pl.BlockSpec((tm, d), row), pl.BlockSpec((tm, d), row)],
        out_shape=[jax.ShapeDtypeStruct((m, d), F32), jax.ShapeDtypeStruct((m, d), BF16)],
        scratch_shapes=[pltpu.VMEM((tm, d), F32)],
        compiler_params=_params(("arbitrary", "arbitrary")),
        name=name,
    )(x16, w_gu, w_gu, w_down, res, g.reshape(1, d), b.reshape(1, d))


def _neg_strict_upper(n):
    r = lax.broadcasted_iota(jnp.int32, (n, n), 0)
    c = lax.broadcasted_iota(jnp.int32, (n, n), 1)
    return jnp.where(r > c, -1.0, 0.0).astype(BF16)


def _sb_soft(z, mask):
    sp = jnp.maximum(z, 0.0) + jnp.log(1.0 + jnp.exp(-jnp.abs(z)))
    log_beta = z - sp
    if mask is not None:
        sp = jnp.where(mask, sp, 0.0)
    return log_beta, sp


def _sb_weight(log_beta, stick, c, mask):
    w = jnp.exp(log_beta + stick + c)
    if mask is not None:
        w = jnp.where(mask, w, 0.0)
    return w.astype(BF16)


def _sb_tiles(qs, biases, blocks, carry, neg_upper, keys_on_lanes=False):
    tiles = [(h, b) for b in range(len(blocks)) for h in range(len(qs))]
    z = {}
    for h, b in tiles:
        kb = blocks[b][0]
        z[h, b] = (_dot(qs[h], kb) if keys_on_lanes else _dot_nt(qs[h], kb)) + biases[h]
    hi, lo, log_beta, c_at = {}, {}, {}, {}
    c_run = [cr[0] for cr in carry]
    for h, b in tiles:
        log_beta[h, b], sp = _sb_soft(z[h, b], blocks[b][2])
        hi[h, b], lo[h, b] = _split2(sp)
        c_at[h, b] = c_run[h]
        c_run[h] = c_run[h] - jnp.sum(sp, axis=1, keepdims=True)
    stick = {}
    upper2 = jnp.concatenate([neg_upper, neg_upper], axis=0)
    for h, b in tiles:
        stick[h, b] = _dot(jnp.concatenate([hi[h, b], lo[h, b]], axis=1), upper2)
    w = {t: _sb_weight(log_beta[t], stick[t], c_at[t], blocks[t[1]][2]) for t in tiles}
    acc = [cr[1] for cr in carry]
    for h, b in tiles:
        vb = blocks[b][1]
        acc[h] = acc[h] + (_dot_nt(w[h, b], vb) if keys_on_lanes else _dot(w[h, b], vb))
    return tuple((c_run[h], acc[h]) for h in range(len(qs)))


def _sb_prompt_kernel(bias_ref, q_ref, k_ref, v_ref, o_ref, kbf, vbf, *, seq, tq, nqb, nh):
    p = pl.program_id(1)
    qi = pl.program_id(2)
    pad = nqb * tq - seq
    hd = SB_HEAD_DIM
    width = nh * hd

    @pl.when(qi == 0)
    def _():
        kbf[0:seq, :] = k_ref[0].astype(BF16)
        v16 = v_ref[0].astype(BF16)
        vhead = lax.broadcasted_iota(jnp.int32, (seq, width), 1) // hd
        for h in range(nh):
            vbf[h, 0:seq, :] = jnp.where(vhead == h, v16, jnp.zeros_like(v16))
        if pad:
            kbf[seq:seq + pad, :] = jnp.zeros((pad, width), BF16)
            for h in range(nh):
                vbf[h, seq:seq + pad, :] = jnp.zeros((pad, width), BF16)

    def run(rows):
        q = q_ref[0, 0:rows, :] * (hd ** -0.5)
        head = lax.broadcasted_iota(jnp.int32, (rows, width), 1) // hd
        rr = lax.broadcasted_iota(jnp.int32, (rows, tq), 0)
        cc = lax.broadcasted_iota(jnp.int32, (rows, tq), 1)
        diag_mask = cc < rr
        neg_upper = _neg_strict_upper(tq)
        zero = jnp.zeros_like(q)
        q_heads = [jnp.where(head == h, q, zero) for h in range(nh)]
        biases = [bias_ref[nh * p + h] for h in range(nh)]

        def block(j, mask):
            off = pl.multiple_of(j * tq, tq)
            return kbf[pl.ds(off, tq), :], [vbf[h, pl.ds(off, tq), :] for h in range(nh)], mask

        def visit(carry, *blocks):
            cs, acc = list(carry[0]), carry[1]
            tiles = [(h, b) for b in range(len(blocks)) for h in range(nh)]
            z = {(h, b): _dot_nt(q_heads[h], blocks[b][0]) + biases[h] for h, b in tiles}
            sp16, log_beta, c_at = {}, {}, {}
            for h, b in tiles:
                log_beta[h, b], sp = _sb_soft(z[h, b], blocks[b][2])
                sp16[h, b] = sp.astype(BF16)
                c_at[h, b] = cs[h]
                cs[h] = cs[h] - jnp.sum(sp, axis=1, keepdims=True)
            stick = {t: _dot(sp16[t], neg_upper) for t in tiles}
            w = {t: _sb_weight(log_beta[t], stick[t], c_at[t], blocks[t[1]][2]) for t in tiles}
            for h, b in tiles:
                acc = acc + _dot(w[h, b], blocks[b][1][h])
            return tuple(cs), acc

        c0 = (jnp.zeros((rows, 1), F32),) * nh
        carry = visit((c0, jnp.zeros((rows, width), F32)), block(qi, diag_mask))
        carry = lax.cond(qi % 2 == 1, lambda cr: visit(cr, block(qi - 1, None)), lambda cr: cr, carry)
        top = qi - 1 - qi % 2
        carry = lax.fori_loop(
            0, qi // 2, lambda i, cr: visit(cr, block(top - 2 * i, None), block(top - 2 * i - 1, None)), carry)
        o_ref[0, 0:rows, :] = carry[1].astype(o_ref.dtype)

    tail = seq - (nqb - 1) * tq
    if tail == tq:
        run(tq)
    else:
        pl.when(qi < nqb - 1)(lambda: run(tq))
        pl.when(qi == nqb - 1)(lambda: run(tail))


def sb_prompt_attention(q16, k32, v32, sb_bias, name):
    bsz, seq, d = q16.shape
    tq = SB_TQ
    nqb = pl.cdiv(seq, tq)
    assert (seq - (nqb - 1) * tq) % SUB_BF16 == 0
    nh = SB_GROUP
    w2 = nh * SB_HEAD_DIM
    kern = functools.partial(_sb_prompt_kernel, seq=seq, tq=tq, nqb=nqb, nh=nh)
    return pl.pallas_call(
        kern,
        grid_spec=pltpu.PrefetchScalarGridSpec(
            num_scalar_prefetch=1,
            grid=(bsz, d // w2, nqb),
            in_specs=[pl.BlockSpec((1, tq, w2), lambda b, p, i, s: (b, i, p)),
                      pl.BlockSpec((1, seq, w2), lambda b, p, i, s: (b, 0, p)),
                      pl.BlockSpec((1, seq, w2), lambda b, p, i, s: (b, 0, p))],
            out_specs=pl.BlockSpec((1, tq, w2), lambda b, p, i, s: (b, i, p)),
            scratch_shapes=[pltpu.VMEM((nqb * tq, w2), BF16), pltpu.VMEM((nh, nqb * tq, w2), BF16)]),
        out_shape=jax.ShapeDtypeStruct((bsz, seq, d), BF16),
        compiler_params=_params(("arbitrary", "arbitrary", "arbitrary")),
        name=name,
    )(sb_bias, q16, k32, v32)


def _sb_sample_kernel(pt_ref, bias_ref, q_ref, kn_ref, vn_ref, *rest, pps, t_new, page):
    k_pages = rest[:pps]
    v_pages = rest[pps:2 * pps]
    o_ref = rest[2 * pps]
    qbd_ref, bias_mat, c_ref, acc_ref = rest[2 * pps + 1:]
    s = pl.program_id(1)
    hd = SB_HEAD_DIM
    d = SB_HEADS * hd
    rows = SB_HEADS * t_new
    neg_upper = _neg_strict_upper(page)

    def visit(blocks, keys_on_lanes):
        ((c, acc),) = _sb_tiles([qbd_ref[...]], [bias_mat[...]], blocks, [(c_ref[...], acc_ref[...])], neg_upper,
                                keys_on_lanes)
        c_ref[...] = c
        acc_ref[...] = acc

    @pl.when(s == 0)
    def _():
        q = q_ref[0] * (hd ** -0.5)
        qrep = jnp.concatenate([q] * SB_HEADS, axis=0)
        rh = lax.broadcasted_iota(jnp.int32, (rows, d), 0) // t_new
        ch = lax.broadcasted_iota(jnp.int32, (rows, d), 1) // hd
        qbd_ref[...] = jnp.where(rh == ch, qrep, 0.0).astype(BF16)
        rh2 = lax.broadcasted_iota(jnp.int32, (rows, page), 0) // t_new
        bm = jnp.zeros((rows, page), F32)
        for h in range(SB_HEADS):
            bm = jnp.where(rh2 == h, bias_ref[h], bm)
        bias_mat[...] = bm
        c_ref[...] = jnp.zeros_like(c_ref)
        acc_ref[...] = jnp.zeros_like(acc_ref)
        zpad = jnp.zeros((page - t_new, d), F32)
        kn = jnp.concatenate([kn_ref[0], zpad], axis=0).astype(BF16)
        vn = jnp.concatenate([vn_ref[0], zpad], axis=0).astype(BF16)
        tq = lax.broadcasted_iota(jnp.int32, (rows, page), 0) % t_new
        sk = lax.broadcasted_iota(jnp.int32, (rows, page), 1)
        visit([(kn, vn, sk < tq)], False)

    visit([(k_pages[r][0].astype(BF16), v_pages[r][0].astype(BF16), None) for r in range(pps)], True)

    @pl.when(s == pl.num_programs(1) - 1)
    def _():
        acc = acc_ref[...]
        ch = lax.broadcasted_iota(jnp.int32, (t_new, d), 1) // hd
        out = jnp.zeros((t_new, d), F32)
        for h in range(SB_HEADS):
            out = out + jnp.where(ch == h, acc[h * t_new:(h + 1) * t_new, :], 0.0)
        o_ref[0] = out


def sb_sample_attention(q, k_new, v_new, cache_k, cache_v, page_table, sb_bias, name):
    bsz, t_new, d = q.shape
    page = cache_k.shape[2]
    n_pages = page_table.shape[1]
    assert page == LANES and SB_HEADS * t_new == LANES
    pps = _pick(n_pages, (16, 8, 4, 2, 1))
    nstep = n_pages // pps
    pt = page_table.reshape(-1).astype(jnp.int32)

    def page_map(r):
        return lambda b, s, pt_ref, bias_ref: (pt_ref[b * n_pages + (n_pages - 1 - (s * pps + r))], 0, 0)

    tok = pl.BlockSpec((1, t_new, d), lambda b, s, *_: (b, 0, 0))
    page_specs = [pl.BlockSpec((1, d, page), page_map(r)) for r in range(pps)]
    kern = functools.partial(_sb_sample_kernel, pps=pps, t_new=t_new, page=page)
    return pl.pallas_call(
        kern,
        grid_spec=pltpu.PrefetchScalarGridSpec(
            num_scalar_prefetch=2,
            grid=(bsz, nstep),
            in_specs=[tok, tok, tok] + page_specs + page_specs,
            out_specs=tok,
            scratch_shapes=[pltpu.VMEM((LANES, d), BF16), pltpu.VMEM((LANES, page), F32),
                            pltpu.VMEM((LANES, page), F32), pltpu.VMEM((LANES, d), F32)]),
        out_shape=jax.ShapeDtypeStruct((bsz, t_new, d), F32),
        compiler_params=_params(("arbitrary", "arbitrary")),
        name=name,
    )(pt, sb_bias, q, k_new, v_new, *([cache_k] * pps), *([cache_v] * pps))


def _gate_kernel(x_ref, wl_ref, w2_ref, bg_ref, o_ref):
    g_low = _dot(x_ref[...].astype(BF16), wl_ref[...].astype(BF16))
    a_hi, a_lo = _split2(g_low)
    w_hi, w_lo = _split2(w2_ref[...])
    pre = _dot(a_hi, w_hi) + _dot(a_hi, w_lo) + _dot(a_lo, w_hi) + bg_ref[...]
    o_ref[...] = -_softplus(-pre) * (1.0 / GLA_TAU)


def gla_log_gate(x, w_low, w_gate2, b_gate, name):
    m, k = x.shape
    rank, kd = w_gate2.shape
    wl = jnp.pad(w_low, ((0, 0), (0, LANES - rank)))
    w2 = jnp.pad(w_gate2, ((0, LANES - rank), (0, 0)))
    tm = _pick(m, (2064, 1024, 512, 256, 128, 16))
    row = lambda i: (i, 0)
    fix = lambda i: (0, 0)
    return pl.pallas_call(
        _gate_kernel,
        grid=(m // tm,),
        in_specs=[pl.BlockSpec((tm, k), row), pl.BlockSpec((k, LANES), fix), pl.BlockSpec((LANES, kd), fix),
                  pl.BlockSpec((1, kd), fix)],
        out_specs=pl.BlockSpec((tm, kd), row),
        out_shape=jax.ShapeDtypeStruct((m, kd), F32),
        compiler_params=_params(("arbitrary",)),
        name=name,
    )(x, wl, w2, b_gate.reshape(1, kd))


def _gla_chunks(qs, ks, vs, lgs, states):
    heads = range(len(qs))
    c, dk = qs[0].shape
    dv = vs[0].shape[1]
    sub = GLA_SUB
    nsub = c // sub
    ri = lax.broadcasted_iota(jnp.int32, (c, c), 0)
    ci = lax.broadcasted_iota(jnp.int32, (c, c), 1)
    lower = jnp.where(ri >= ci, 1.0, 0.0).astype(BF16)
    ones_c = jnp.ones((c, dk), BF16)
    ones_k = jnp.ones((dk, LANES), BF16)
    row16 = lax.broadcasted_iota(jnp.int32, (sub, dk), 0)
    lane16 = lax.broadcasted_iota(jnp.int32, (sub, LANES), 1)
    rowc = lax.broadcasted_iota(jnp.int32, (c, dk), 0)
    zrows = jnp.zeros((LANES - c, dk), BF16) if c < LANES else None

    parts = [_split3(lgs[h]) for h in heads]
    b = [_dot(lower, parts[h][0]) + _dot(lower, parts[h][1]) + _dot(lower, parts[h][2]) for h in heads]
    b_end_col = [_dot_tn(parts[h][0], ones_c) + _dot_tn(parts[h][1], ones_c) + _dot_tn(parts[h][2], ones_c)
                 for h in heads]

    q_in = [(qs[h] * jnp.exp(b[h])).astype(BF16) for h in heads]
    k_end = [(ks[h] * jnp.exp(b[h][c - 1:c, :] - b[h])).astype(BF16) for h in heads]
    o = [_dot(q_in[h], states[h].astype(BF16)) for h in heads]
    kv = [_dot_tn(k_end[h], vs[h]) for h in heads]
    new_states = [jnp.concatenate([jnp.exp(b_end_col[h])] * (dv // dk), axis=1) * states[h] + kv[h] for h in heads]

    pieces = []
    for h in heads:
        ph = []
        for i in range(nsub):
            qsub = qs[h][i * sub:(i + 1) * sub]
            ksub = ks[h][i * sub:(i + 1) * sub]
            bsub = b[h][i * sub:(i + 1) * sub]
            for s in range(sub):
                e = jnp.exp(jnp.where(row16 >= s, bsub - bsub[s:s + 1, :], -1e30))
                ph.append((qsub * e * ksub[s:s + 1, :]).astype(BF16))
        pieces.append(jnp.concatenate(ph, axis=0))
    pair = [_dot(pieces[h], ones_k) for h in heads]
    qq, kk = {}, {}
    for h in heads:
        for i in range(1, nsub):
            b_ref = b[h][i * sub - 1:i * sub, :]
            qq[h, i] = (qs[h][i * sub:(i + 1) * sub] * jnp.exp(b[h][i * sub:(i + 1) * sub] - b_ref)).astype(BF16)
            kki = jnp.where(rowc < i * sub, ks[h] * jnp.exp(jnp.minimum(b_ref - b[h], 0.0)), 0.0).astype(BF16)
            kk[h, i] = kki if zrows is None else jnp.concatenate([kki, zrows], axis=0)
    cross = {key: _dot_nt(qq[key], kk[key]) for key in qq}
    scores = []
    for h in heads:
        rows = []
        for i in range(nsub):
            sc = jnp.zeros((sub, LANES), F32)
            for s in range(sub):
                blk = pair[h][(i * sub + s) * sub:(i * sub + s + 1) * sub]
                sc = jnp.where(lane16 == i * sub + s, blk, sc)
            if i > 0:
                sc = sc + cross[h, i]
            rows.append(sc)
        scores.append(jnp.concatenate(rows, axis=0).astype(BF16))
    vpad = [vs[h] if c == LANES else jnp.concatenate([vs[h], jnp.zeros((LANES - c, dv), BF16)], axis=0)
            for h in heads]
    o = [o[h] + _dot(scores[h], vpad[h]) for h in heads]
    return o, new_states


def _gla_finish(o, r, gn):
    ms = jnp.mean(o * o, axis=-1, keepdims=True)
    return o * lax.rsqrt(ms + RMS_EPS) * gn * (r * _sigmoid(r))


def _gla_prompt_kernel(q_ref, k_ref, lg_ref, v_ref, r_ref, gn_ref, og_ref, s_ref, state):
    scale = GLA_HEAD_K ** -0.5
    dk, dv, nh = GLA_HEAD_K, GLA_HEAD_V, GLA_HEADS
    ci = pl.program_id(1)

    @pl.when(ci == 0)
    def _():
        state[...] = jnp.zeros_like(state)

    kslice = [slice(h * dk, (h + 1) * dk) for h in range(nh)]
    vslice = [slice(h * dv, (h + 1) * dv) for h in range(nh)]
    o, s_new = _gla_chunks([q_ref[0, :, ks] * scale for ks in kslice], [k_ref[0, :, ks] for ks in kslice],
                           [v_ref[0, :, vs] for vs in vslice], [lg_ref[0, :, ks] for ks in kslice],
                           [state[h] for h in range(nh)])
    gn = gn_ref[...]
    for h in range(nh):
        state[h] = s_new[h]
        og_ref[0, :, vslice[h]] = _gla_finish(o[h], r_ref[0, :, vslice[h]].astype(F32), gn).astype(og_ref.dtype)

    @pl.when(ci == pl.num_programs(1) - 1)
    def _():
        s_ref[0] = state[...]


def gla_prompt(q, k, lg, v16, r16, gnorm_g, name):
    bsz, seq, _ = q.shape
    dk, dv, nh = GLA_HEAD_K, GLA_HEAD_V, GLA_HEADS
    chunk = _pick(seq, GLA_CHUNKS)
    kspec = pl.BlockSpec((1, chunk, nh * dk), lambda b, c: (b, c, 0))
    vspec = pl.BlockSpec((1, chunk, nh * dv), lambda b, c: (b, c, 0))
    return pl.pallas_call(
        _gla_prompt_kernel,
        grid=(bsz, seq // chunk),
        in_specs=[kspec, kspec, kspec, vspec, vspec, pl.BlockSpec((1, dv), lambda b, c: (0, 0))],
        out_specs=[vspec, pl.BlockSpec((1, nh, dk, dv), lambda b, c: (b, 0, 0, 0))],
        out_shape=[jax.ShapeDtypeStruct((bsz, seq, nh * dv), BF16),
                   jax.ShapeDtypeStruct((bsz, nh, dk, dv), F32)],
        scratch_shapes=[pltpu.VMEM((nh, dk, dv), F32)],
        compiler_params=_params(("arbitrary", "arbitrary")),
        name=name,
    )(q, k, lg, v16, r16, gnorm_g.reshape(1, dv))


def _gla_sample_kernel(q_ref, k_ref, lg_ref, v_ref, r_ref, gn_ref, s0_ref, og_ref, s_ref, *, t_new):
    scale = GLA_HEAD_K ** -0.5
    dk, dv, nh = GLA_HEAD_K, GLA_HEAD_V, GLA_HEADS
    padk = jnp.zeros((GLA_SUB - t_new, dk), F32)
    padv = jnp.zeros((GLA_SUB - t_new, dv), F32)
    kslice = [slice(h * dk, (h + 1) * dk) for h in range(nh)]
    vslice = [slice(h * dv, (h + 1) * dv) for h in range(nh)]
    o, s_new = _gla_chunks(
        [jnp.concatenate([q_ref[0, :, ks] * scale, padk], axis=0) for ks in kslice],
        [jnp.concatenate([k_ref[0, :, ks], padk], axis=0) for ks in kslice],
        [jnp.concatenate([v_ref[0, :, vs], padv], axis=0).astype(BF16) for vs in vslice],
        [jnp.concatenate([lg_ref[0, :, ks], padk], axis=0) for ks in kslice],
        [s0_ref[0, h] for h in range(nh)])
    gn = gn_ref[...]
    for h in range(nh):
        s_ref[0, h] = s_new[h]
        og_ref[0, :, vslice[h]] = _gla_finish(o[h][0:t_new], r_ref[0, :, vslice[h]], gn)


def gla_sample(q, k, lg, v, r, gnorm_g, state, name):
    bsz, t_new, _ = q.shape
    dk, dv, nh = GLA_HEAD_K, GLA_HEAD_V, GLA_HEADS
    assert t_new <= GLA_SUB
    kspec = pl.BlockSpec((1, t_new, nh * dk), lambda b: (b, 0, 0))
    vspec = pl.BlockSpec((1, t_new, nh * dv), lambda b: (b, 0, 0))
    sspec = pl.BlockSpec((1, nh, dk, dv), lambda b: (b, 0, 0, 0))
    kern = functools.partial(_gla_sample_kernel, t_new=t_new)
    return pl.pallas_call(
        kern,
        grid=(bsz,),
        in_specs=[kspec, kspec, kspec, vspec, vspec, pl.BlockSpec((1, dv), lambda b: (0, 0)), sspec],
        out_specs=[vspec, sspec],
        out_shape=[jax.ShapeDtypeStruct((bsz, t_new, nh * dv), F32),
                   jax.ShapeDtypeStruct((bsz, nh, dk, dv), F32)],
        compiler_params=_params(("arbitrary",)),
        name=name,
    )(q, k, lg, v, r, gnorm_g.reshape(1, dv), state)


def _router_kernel(h_ref, wr_ref, meta_ref, cnt_ref, carry, *, n_tok, tm):
    i = pl.program_id(0)

    @pl.when(i == 0)
    def _():
        carry[...] = jnp.zeros_like(carry)

    hrow = lax.broadcasted_iota(jnp.int32, h_ref.shape, 0)
    h_hi, h_lo = _split2(jnp.where(i * tm + hrow < n_tok, h_ref[...], 0.0))
    w_hi, w_lo = _split2(wr_ref[...])
    logits = _dot(h_hi, w_hi) + _dot(h_lo, w_hi) + _dot(h_hi, w_lo)
    lane = lax.broadcasted_iota(jnp.int32, (tm, LANES), 1)
    row = lax.broadcasted_iota(jnp.int32, (tm, LANES), 0)
    neg = -jnp.inf
    l1 = jnp.where(lane < N_EXPERTS, logits, neg)
    m1 = jnp.max(l1, axis=1, keepdims=True)
    i1 = jnp.min(jnp.where(l1 == m1, lane, LANES), axis=1, keepdims=True)
    l2 = jnp.where(lane == i1, neg, l1)
    m2 = jnp.max(l2, axis=1, keepdims=True)
    i2 = jnp.min(jnp.where(l2 == m2, lane, LANES), axis=1, keepdims=True)
    e = jnp.exp(m2 - m1)
    w1 = 1.0 / (1.0 + e)
    w2 = e * w1
    valid = (i * tm + row) < n_tok
    hit = jnp.where(valid & ((lane == i1) | (lane == i2)), 1.0, 0.0)
    ri = lax.broadcasted_iota(jnp.int32, (tm, tm), 0)
    ci = lax.broadcasted_iota(jnp.int32, (tm, tm), 1)
    before = jnp.where(ri > ci, 1.0, 0.0).astype(BF16)
    rank = _dot(before, hit.astype(BF16)) + carry[0:1, :]
    r1 = jnp.sum(jnp.where(lane == i1, rank, 0.0), axis=1, keepdims=True)
    r2 = jnp.sum(jnp.where(lane == i2, rank, 0.0), axis=1, keepdims=True)
    carry[0:1, :] = carry[0:1, :] + jnp.sum(hit, axis=0, keepdims=True)
    meta = jnp.where(lane == 0, i1.astype(F32), 0.0)
    meta = jnp.where(lane == 1, i2.astype(F32), meta)
    meta = jnp.where(lane == 2, w1, meta)
    meta = jnp.where(lane == 3, w2, meta)
    meta = jnp.where(lane == 4, r1, meta)
    meta = jnp.where(lane == 5, r2, meta)
    meta_ref[...] = meta

    @pl.when(i == pl.num_programs(0) - 1)
    def _():
        cnt_ref[...] = carry[...]


def moe_route(h32, w_router):
    n_tok, d = h32.shape
    tm = ROUTE_TM
    nt = pl.cdiv(n_tok, tm)
    wr = jnp.pad(w_router, ((0, 0), (0, LANES - w_router.shape[1])))
    kern = functools.partial(_router_kernel, n_tok=n_tok, tm=tm)
    return pl.pallas_call(
        kern,
        grid=(nt,),
        in_specs=[pl.BlockSpec((tm, d), lambda i: (i, 0)), pl.BlockSpec((d, LANES), lambda i: (0, 0))],
        out_specs=[pl.BlockSpec((tm, LANES), lambda i: (i, 0)), pl.BlockSpec((8, LANES), lambda i: (0, 0))],
        out_shape=[jax.ShapeDtypeStruct((nt * tm, LANES), F32), jax.ShapeDtypeStruct((8, LANES), F32)],
        scratch_shapes=[pltpu.VMEM((8, LANES), F32)],
        compiler_params=_params(("arbitrary",)),
        name="moe_route",
    )(h32, wr)


def _row_copy(src, dst, i, j, sem):
    return pltpu.make_async_copy(src.at[pl.ds(i, 1)], dst.at[pl.ds(j, 1)], sem)


def _dispatch_kernel(pos_ref, h_ref, xs_in, xs_out, sem, *, tm):
    del xs_in

    def for_each(fn):
        def body(r, carry):
            for slot in range(2):
                fn(_row_copy(h_ref, xs_out, r, pos_ref[0, 0, 2 * r + slot], sem))
            return carry
        lax.fori_loop(0, tm, body, 0, unroll=DMA_UNROLL)

    for_each(lambda cp: cp.start())
    for_each(lambda cp: cp.wait())


def moe_dispatch(h32, pos, n_rows):
    n_tok, d = h32.shape
    tm = _row_tile(n_tok, 1536)
    nt = n_tok // tm
    pos = pos.reshape(nt, 1, 2 * tm)
    kern = functools.partial(_dispatch_kernel, tm=tm)
    any_spec = pl.BlockSpec(memory_space=pl.ANY)
    return pl.pallas_call(
        kern,
        grid=(nt,),
        in_specs=[pl.BlockSpec((1, 1, 2 * tm), lambda i: (i, 0, 0), memory_space=pltpu.SMEM),
                  pl.BlockSpec((tm, d), lambda i: (i, 0)), any_spec],
        out_specs=any_spec,
        out_shape=jax.ShapeDtypeStruct((n_rows, d), F32),
        scratch_shapes=[pltpu.SemaphoreType.DMA(())],
        input_output_aliases={2: 0},
        compiler_params=_params(("arbitrary",)),
        name="moe_dispatch",
    )(pos, h32, jnp.zeros((n_rows, d), F32))


def _experts_kernel(te_ref, tv_ref, x_ref, wg_ref, wu_ref, wd_ref, y_ref, xbf, acc):
    t = pl.program_id(0)
    f = pl.program_id(1)
    live = tv_ref[t] > 0

    @pl.when(f == 0)
    def _():
        xbf[...] = x_ref[...].astype(BF16)
        acc[...] = jnp.zeros_like(acc)

    @pl.when(live)
    def _():
        x = xbf[...]
        gate = _dot(x, wg_ref[0].astype(BF16))
        up = _dot(x, wu_ref[0].astype(BF16))
        act = (gate * _sigmoid(gate) * up).astype(BF16)
        acc[...] += _dot(act, wd_ref[0].astype(BF16))

    @pl.when(f == pl.num_programs(1) - 1)
    def _():
        y_ref[...] = acc[...]


def moe_experts(xs, w_gu_e, w_down_e, tile_expert, tile_live):
    n_rows, d = xs.shape
    dff = w_down_e.shape[1]
    tm, tf = MOE_TM, MOE_TF
    nf = dff // tf
    assert nf * tf == dff and n_rows % tm == 0

    def fcol(t, f, tv):
        return jnp.where(tv[t] > 0, f, nf - 1)

    return pl.pallas_call(
        _experts_kernel,
        grid_spec=pltpu.PrefetchScalarGridSpec(
            num_scalar_prefetch=2,
            grid=(n_rows // tm, nf),
            in_specs=[pl.BlockSpec((tm, d), lambda t, f, te, tv: (t, 0)),
                      pl.BlockSpec((1, d, tf), lambda t, f, te, tv: (te[t], 0, fcol(t, f, tv))),
                      pl.BlockSpec((1, d, tf), lambda t, f, te, tv: (te[t], 0, nf + fcol(t, f, tv))),
                      pl.BlockSpec((1, tf, d), lambda t, f, te, tv: (te[t], fcol(t, f, tv), 0))],
            out_specs=pl.BlockSpec((tm, d), lambda t, f, te, tv: (t, 0)),
            scratch_shapes=[pltpu.VMEM((tm, d), BF16), pltpu.VMEM((tm, d), F32)]),
        out_shape=jax.ShapeDtypeStruct((n_rows, d), F32),
        compiler_params=_params(("arbitrary", "arbitrary")),
        name="moe_experts",
    )(tile_expert, tile_live, xs, w_gu_e, w_gu_e, w_down_e)


def _combine_kernel(pos_ref, meta_ref, res_ref, g_ref, b_ref, y_hbm, o32_ref, o16_ref, y1, y2, sem, *, tm):
    def for_each(fn):
        def body(r, carry):
            fn(_row_copy(y_hbm, y1, pos_ref[0, 0, 2 * r], r, sem))
            fn(_row_copy(y_hbm, y2, pos_ref[0, 0, 2 * r + 1], r, sem))
            return carry
        lax.fori_loop(0, tm, body, 0, unroll=DMA_UNROLL)

    for_each(lambda cp: cp.start())
    for_each(lambda cp: cp.wait())
    meta = meta_ref[...]
    mix = meta[:, 2:3] * y1[...] + meta[:, 3:4] * y2[...]
    h = _layer_norm(DEEPNORM_ALPHA * res_ref[...] + mix, g_ref[...], b_ref[...])
    o32_ref[...] = h
    o16_ref[...] = h.astype(BF16)


def moe_combine_deepnorm(ys, pos, meta, res, g, b, name):
    m, d = res.shape
    tm = _pick(m, (688, 512, 256, 128, 16))
    nt = m // tm
    row = lambda i: (i, 0)
    fix = lambda i: (0, 0)
    kern = functools.partial(_combine_kernel, tm=tm)
    return pl.pallas_call(
        kern,
        grid=(nt,),
        in_specs=[pl.BlockSpec((1, 1, 2 * tm), lambda i: (i, 0, 0), memory_space=pltpu.SMEM),
                  pl.BlockSpec((tm, LANES), row), pl.BlockSpec((tm, d), row),
                  pl.BlockSpec((1, d), fix), pl.BlockSpec((1, d), fix), pl.BlockSpec(memory_space=pl.ANY)],
        out_specs=[pl.BlockSpec((tm, d), row), pl.BlockSpec((tm, d), row)],
        out_shape=[jax.ShapeDtypeStruct((m, d), F32), jax.ShapeDtypeStruct((m, d), BF16)],
        scratch_shapes=[pltpu.VMEM((tm, d), F32), pltpu.VMEM((tm, d), F32), pltpu.SemaphoreType.DMA(())],
        compiler_params=_params(("arbitrary",)),
        name=name,
    )(pos.reshape(nt, 1, 2 * tm), meta, res, g.reshape(1, d), b.reshape(1, d), ys)


def moe_deepnorm(hp32, hs32, w_router, w_gu_e, w_down_e, g, b, layer):
    mp = hp32.shape[0]
    h_all = jnp.concatenate([hp32, hs32], axis=0)
    n_tok = h_all.shape[0]
    meta, cnt = moe_route(h_all, w_router)
    meta = meta[:n_tok]
    counts = cnt[0, :N_EXPERTS].astype(jnp.int32)
    padded = ((counts + MOE_TM - 1) // MOE_TM) * MOE_TM
    ends = jnp.cumsum(padded)
    starts = ends - padded
    e1 = meta[:, 0].astype(jnp.int32)
    e2 = meta[:, 1].astype(jnp.int32)
    eids = jnp.arange(N_EXPERTS, dtype=jnp.int32)
    start1 = jnp.sum(jnp.where(e1[:, None] == eids[None, :], starts[None, :], 0), axis=1)
    start2 = jnp.sum(jnp.where(e2[:, None] == eids[None, :], starts[None, :], 0), axis=1)
    pos = jnp.stack([start1 + meta[:, 4].astype(jnp.int32), start2 + meta[:, 5].astype(jnp.int32)], axis=1)
    n_tiles = pl.cdiv(2 * n_tok, MOE_TM) + N_EXPERTS
    tile_start = jnp.arange(n_tiles, dtype=jnp.int32) * MOE_TM
    tile_live = (tile_start < ends[-1]).astype(jnp.int32)
    tile_expert = jnp.sum((tile_start[:, None] >= ends[None, :]).astype(jnp.int32), axis=1)
    last_live = jnp.sum((jnp.maximum(ends[-1] - 1, 0) >= ends).astype(jnp.int32))
    tile_expert = jnp.where(tile_live > 0, tile_expert, last_live).astype(jnp.int32)

    xs = moe_dispatch(h_all, pos, n_tiles * MOE_TM)
    ys = moe_experts(xs, w_gu_e, w_down_e, tile_expert, tile_live)
    outp = moe_combine_deepnorm(ys, pos[:mp], meta[:mp], hp32, g, b, f"l{layer}_moe_combine_p")
    outs = moe_combine_deepnorm(ys, pos[mp:], meta[mp:], hs32, g, b, f"l{layer}_moe_combine_s")
    return outp, outs


def _sb_layer(i, hp32, hp16, hs32, hs16, bsz, seq, dec_b, dec_t, cache_k, cache_v, page_table,
              w_qkv, w_o, sb_bias, ln1_g, ln1_b):
    d = D_MODEL
    tag = f"l{i}_"
    qp = linear(hp16, w_qkv, 0, d, BF16, tag + "q_p")
    kp = linear(hp16, w_qkv, d, d, F32, tag + "k_p")
    vp = linear(hp16, w_qkv, 2 * d, d, F32, tag + "v_p")
    qs = linear(hs16, w_qkv, 0, d, F32, tag + "q_s")
    ks = linear(hs16, w_qkv, d, d, F32, tag + "k_s")
    vs = linear(hs16, w_qkv, 2 * d, d, F32, tag + "v_s")
    op = sb_prompt_attention(qp.reshape(bsz, seq, d), kp.reshape(bsz, seq, d), vp.reshape(bsz, seq, d),
                             sb_bias, tag + "sb_p")
    pool, page = cache_k.shape[0], cache_k.shape[1]
    ck = jnp.transpose(cache_k, (0, 2, 3, 1)).reshape(pool, d, page)
    cv = jnp.transpose(cache_v, (0, 2, 3, 1)).reshape(pool, d, page)
    os_ = sb_sample_attention(qs.reshape(dec_b, dec_t, d), ks.reshape(dec_b, dec_t, d), vs.reshape(dec_b, dec_t, d),
                              ck, cv, page_table, sb_bias, tag + "sb_s")
    hp32, hp16 = linear_deepnorm(op.reshape(bsz * seq, d), w_o, hp32, ln1_g, ln1_b, tag + "o_p")
    hs32, hs16 = linear_deepnorm(os_.reshape(dec_b * dec_t, d), w_o, hs32, ln1_g, ln1_b, tag + "o_s")
    shp = (bsz, seq, SB_HEADS, SB_HEAD_DIM)
    shs = (dec_b, dec_t, SB_HEADS, SB_HEAD_DIM)
    new_state = [kp.reshape(shp), vp.reshape(shp), ks.reshape(shs), vs.reshape(shs)]
    return hp32, hp16, hs32, hs16, new_state


def _gla_layer(i, hp32, hp16, hs32, hs16, bsz, seq, dec_b, dec_t, state,
               w_in, w_gate2, b_gate, gnorm_g, w_o, ln1_g, ln1_b):
    tag = f"l{i}_"
    kd, vd = GLA_KEY_DIM, GLA_HEADS * GLA_HEAD_V
    w_low = w_in[:, 2 * kd + 2 * vd:]

    def project(x16, vdtype, sfx):
        q = linear(x16, w_in, 0, kd, F32, tag + "q" + sfx)
        k = linear(x16, w_in, kd, kd, F32, tag + "k" + sfx)
        v = linear(x16, w_in, 2 * kd, vd, vdtype, tag + "v" + sfx)
        r = linear(x16, w_in, 2 * kd + vd, vd, vdtype, tag + "r" + sfx)
        lg = gla_log_gate(x16, w_low, w_gate2, b_gate, tag + "gate" + sfx)
        return q, k, lg, v, r

    q, k, lg, v, r = project(hp16, BF16, "_p")
    og_p, sp = gla_prompt(q.reshape(bsz, seq, kd), k.reshape(bsz, seq, kd), lg.reshape(bsz, seq, kd),
                          v.reshape(bsz, seq, vd), r.reshape(bsz, seq, vd), gnorm_g, tag + "gla_p")
    q, k, lg, v, r = project(hs16, F32, "_s")
    og_s, ss = gla_sample(q.reshape(dec_b, dec_t, kd), k.reshape(dec_b, dec_t, kd), lg.reshape(dec_b, dec_t, kd),
                          v.reshape(dec_b, dec_t, vd), r.reshape(dec_b, dec_t, vd), gnorm_g, state, tag + "gla_s")
    hp32, hp16 = linear_deepnorm(og_p.reshape(bsz * seq, vd), w_o, hp32, ln1_g, ln1_b, tag + "o_p")
    hs32, hs16 = linear_deepnorm(og_s.reshape(dec_b * dec_t, vd), w_o, hs32, ln1_g, ln1_b, tag + "o_s")
    return hp32, hp16, hs32, hs16, [sp, ss]


def kernel(x_prompt, x_sample, cache_k0, cache_v0, state_s1, cache_k2, cache_v2, state_s3, page_table, meta_tokens, l0_w_qkv, l0_w_o, l0_sb_bias, l0_ln1_g, l0_ln1_b, l0_w_gu, l0_w_down, l0_ln2_g, l0_ln2_b, l1_w_in, l1_w_gate2, l1_b_gate, l1_gnorm_g, l1_w_o, l1_ln1_g, l1_ln1_b, l1_w_router, l1_w_gu_e, l1_w_down_e, l1_ln2_g, l1_ln2_b, l2_w_qkv, l2_w_o, l2_sb_bias, l2_ln1_g, l2_ln1_b, l2_w_gu, l2_w_down, l2_ln2_g, l2_ln2_b, l3_w_in, l3_w_gate2, l3_b_gate, l3_gnorm_g, l3_w_o, l3_ln1_g, l3_ln1_b, l3_w_router, l3_w_gu_e, l3_w_down_e, l3_ln2_g, l3_ln2_b):
    bsz, seq0, d = x_prompt.shape
    dec_b, dec_t, _ = x_sample.shape
    n_meta = meta_tokens.shape[0]
    seq = n_meta + seq0
    meta = jnp.broadcast_to(meta_tokens.astype(x_prompt.dtype)[None], (bsz, n_meta, d))
    hp32 = jnp.concatenate([meta, x_prompt], axis=1).reshape(bsz * seq, d)
    hs32 = x_sample.reshape(dec_b * dec_t, d)
    hp16 = hp32.astype(BF16)
    hs16 = hs32.astype(BF16)

    sb = [(cache_k0, cache_v0, l0_w_qkv, l0_w_o, l0_sb_bias, l0_ln1_g, l0_ln1_b),
          (cache_k2, cache_v2, l2_w_qkv, l2_w_o, l2_sb_bias, l2_ln1_g, l2_ln1_b)]
    ffn = [(l0_w_gu, l0_w_down, l0_ln2_g, l0_ln2_b), (l2_w_gu, l2_w_down, l2_ln2_g, l2_ln2_b)]
    gla = [(state_s1, l1_w_in, l1_w_gate2, l1_b_gate, l1_gnorm_g, l1_w_o, l1_ln1_g, l1_ln1_b),
           (state_s3, l3_w_in, l3_w_gate2, l3_b_gate, l3_gnorm_g, l3_w_o, l3_ln1_g, l3_ln1_b)]
    moe = [(l1_w_router, l1_w_gu_e, l1_w_down_e, l1_ln2_g, l1_ln2_b),
           (l3_w_router, l3_w_gu_e, l3_w_down_e, l3_ln2_g, l3_ln2_b)]

    new_state = []
    for i in range(DEPTH):
        if i % 2 == 0:
            cache_k, cache_v, w_qkv, w_o, sb_bias, g1, b1 = sb[i // 2]
            hp32, hp16, hs32, hs16, st = _sb_layer(i, hp32, hp16, hs32, hs16, bsz, seq, dec_b, dec_t,
                                                   cache_k, cache_v, page_table, w_qkv, w_o, sb_bias, g1, b1)
            w_gu, w_down, g2, b2 = ffn[i // 2]
            hp32, hp16 = swiglu_deepnorm(hp16, w_gu, w_down, hp32, g2, b2, f"l{i}_ffn_p")
            hs32, hs16 = swiglu_deepnorm(hs16, w_gu, w_down, hs32, g2, b2, f"l{i}_ffn_s")
        else:
            state, w_in, w_gate2, b_gate, gnorm_g, w_o, g1, b1 = gla[i // 2]
            hp32, hp16, hs32, hs16, st = _gla_layer(i, hp32, hp16, hs32, hs16, bsz, seq, dec_b, dec_t, state,
                                                    w_in, w_gate2, b_gate, gnorm_g, w_o, g1, b1)
            w_router, w_gu_e, w_down_e, g2, b2 = moe[i // 2]
            (hp32, hp16), (hs32, hs16) = moe_deepnorm(hp32, hs32, w_router, w_gu_e, w_down_e, g2, b2, i)
        new_state += st

    y_prompt = hp32.reshape(bsz, seq, d)[:, n_meta:]
    y_sample = hs32.reshape(dec_b, dec_t, d)
    return (y_prompt, y_sample, *new_state)
```

```python
import functools

import jax
import jax.numpy as jnp
from jax import lax
from jax.experimental import pallas as pl
from jax.experimental.pallas import tpu as pltpu

F32 = jnp.float32
BF16 = jnp.bfloat16

D_MODEL = 1024
N_META = 16
SB_HEADS = 16
SB_HEAD_DIM = 64
GLA_HEADS = 4
GLA_HEAD_K = 128
GLA_HEAD_V = 256
GLA_KEY_DIM = GLA_HEADS * GLA_HEAD_K
GLA_GATE_RANK = 16
GLA_TAU = 16.0
N_EXPERTS = 8
DEPTH = 4
DEEPNORM_ALPHA = float((2 * DEPTH) ** 0.25)
LN_EPS = 1e-5
RMS_EPS = 1e-6

LANES = 128
SUB_BF16 = 16
VMEM_LIMIT = 56 * 1024 * 1024

SB_TQ = 256
SB_GROUP = 4
GLA_SUB = 16
GLA_CHUNKS = (48, 64, 32, 16)
MOE_TM = 1024
MOE_TF = 512
ROUTE_TM = 1024
DMA_UNROLL = 4
FFN_TF = 256


def _params(sem):
    return pltpu.CompilerParams(dimension_semantics=sem, vmem_limit_bytes=VMEM_LIMIT)


def _row_tile(m, cap):
    for t in range(cap - cap % 8, 0, -8):
        if m % t == 0:
            return t
    raise ValueError(f"no row tile for {m}")


def _pick(m, cands):
    for c in cands:
        if m % c == 0:
            return c
    raise ValueError(f"no tile for {m} in {cands}")


def _dot(a, b):
    return jnp.dot(a, b, preferred_element_type=F32)


def _dot_nt(a, b):
    return lax.dot_general(a, b, (((1,), (1,)), ((), ())), preferred_element_type=F32)


def _dot_tn(a, b):
    return lax.dot_general(a, b, (((0,), (0,)), ((), ())), preferred_element_type=F32)


def _softplus(z):
    return jnp.maximum(z, 0.0) + jnp.log1p(jnp.exp(-jnp.abs(z)))


def _sigmoid(x):
    return 1.0 / (1.0 + jnp.exp(-x))


def _split2(x):
    hi = x.astype(BF16)
    lo = (x - hi.astype(F32)).astype(BF16)
    return hi, lo


def _split3(x):
    a = x.astype(BF16)
    r = x - a.astype(F32)
    b = r.astype(BF16)
    c = (r - b.astype(F32)).astype(BF16)
    return a, b, c


def _layer_norm(y, g, b):
    mu = jnp.mean(y, axis=-1, keepdims=True)
    yc = y - mu
    var = jnp.mean(yc * yc, axis=-1, keepdims=True)
    return yc * lax.rsqrt(var + LN_EPS) * g + b


def _linear_kernel(x_ref, w_ref, o_ref, wbf_ref):
    @pl.when(pl.program_id(1) == 0)
    def _():
        wbf_ref[...] = w_ref[...].astype(BF16)

    o_ref[...] = _dot(x_ref[...].astype(BF16), wbf_ref[...]).astype(o_ref.dtype)


def linear(x, w, col0, ncols, out_dtype, name):
    m, k = x.shape
    tm = _pick(m, (2064, 1024, 512, 256, 128, 16))
    tn = _pick(ncols, (1024, 512, 256, 128))
    assert col0 % tn == 0
    return pl.pallas_call(
        _linear_kernel,
        grid=(ncols // tn, m // tm),
        in_specs=[pl.BlockSpec((tm, k), lambda j, i: (i, 0)),
                  pl.BlockSpec((k, tn), lambda j, i: (0, col0 // tn + j))],
        out_specs=pl.BlockSpec((tm, tn), lambda j, i: (i, j)),
        out_shape=jax.ShapeDtypeStruct((m, ncols), out_dtype),
        scratch_shapes=[pltpu.VMEM((k, tn), BF16)],
        compiler_params=_params(("arbitrary", "arbitrary")),
        name=name,
    )(x, w)


def _linear_ln_kernel(x_ref, w_ref, res_ref, g_ref, b_ref, o32_ref, o16_ref, wbf_ref):
    @pl.when(pl.program_id(0) == 0)
    def _():
        wbf_ref[...] = w_ref[...].astype(BF16)

    y = _dot(x_ref[...].astype(BF16), wbf_ref[...])
    h = _layer_norm(DEEPNORM_ALPHA * res_ref[...] + y, g_ref[...], b_ref[...])
    o32_ref[...] = h
    o16_ref[...] = h.astype(BF16)


def linear_deepnorm(x, w, res, g, b, name):
    m, k = x.shape
    d = w.shape[1]
    tm = _pick(m, (688, 512, 256, 128, 16))
    row = lambda i: (i, 0)
    fix = lambda i: (0, 0)
    return pl.pallas_call(
        _linear_ln_kernel,
        grid=(m // tm,),
        in_specs=[pl.BlockSpec((tm, k), row), pl.BlockSpec((k, d), fix), pl.BlockSpec((tm, d), row),
                  pl.BlockSpec((1, d), fix), pl.BlockSpec((1, d), fix)],
        out_specs=[pl.BlockSpec((tm, d), row), pl.BlockSpec((tm, d), row)],
        out_shape=[jax.ShapeDtypeStruct((m, d), F32), jax.ShapeDtypeStruct((m, d), BF16)],
        scratch_shapes=[pltpu.VMEM((k, d), BF16)],
        compiler_params=_params(("arbitrary",)),
        name=name,
    )(x, w, res, g.reshape(1, d), b.reshape(1, d))


def _swiglu_ln_kernel(x_ref, wg_ref, wu_ref, wd_ref, res_ref, g_ref, b_ref, o32_ref, o16_ref, acc_ref):
    f = pl.program_id(1)

    @pl.when(f == 0)
    def _():
        acc_ref[...] = jnp.zeros_like(acc_ref)

    x = x_ref[...]
    gate = _dot(x, wg_ref[...].astype(BF16))
    up = _dot(x, wu_ref[...].astype(BF16))
    act = (gate * _sigmoid(gate) * up).astype(BF16)
    acc_ref[...] += _dot(act, wd_ref[...].astype(BF16))

    @pl.when(f == pl.num_programs(1) - 1)
    def _():
        h = _layer_norm(DEEPNORM_ALPHA * res_ref[...] + acc_ref[...], g_ref[...], b_ref[...])
        o32_ref[...] = h
        o16_ref[...] = h.astype(BF16)


def swiglu_deepnorm(x16, w_gu, w_down, res, g, b, name):
    m, d = x16.shape
    dff = w_down.shape[0]
    tm = _pick(m, (1376, 512, 256, 128, 16))
    tf = FFN_TF
    nf = dff // tf
    assert nf * tf == dff
    row = lambda i, f: (i, 0)
    fix = lambda i, f: (0, 0)
    return pl.pallas_call(
        _swiglu_ln_kernel,
        grid=(m // tm, nf),
        in_specs=[pl.BlockSpec((tm, d), row),
                  pl.BlockSpec((d, tf), lambda i, f: (0, f)),
                  pl.BlockSpec((d, tf), lambda i, f: (0, nf + f)),
                  pl.BlockSpec((tf, d), lambda i, f: (f, 0)),
                  pl.BlockSpec((tm, d), row), pl.BlockSpec((1, d), fix), pl.BlockSpec((1, d), fix)],
        out_specs=[pl.BlockSpec((tm, d), row), pl.BlockSpec((tm, d), row)],
        out_shape=[jax.ShapeDtypeStruct((m, d), F32), jax.ShapeDtypeStruct((m, d), BF16)],
        scratch_shapes=[pltpu.VMEM((tm, d), F32)],
        compiler_params=_params(("arbitrary", "arbitrary")),
        name=name,
    )(x16, w_gu, w_gu, w_down, res, g.reshape(1, d), b.reshape(1, d))


def _neg_strict_upper(n):
    r = lax.broadcasted_iota(jnp.int32, (n, n), 0)
    c = lax.broadcasted_iota(jnp.int32, (n, n), 1)
    return jnp.where(r > c, -1.0, 0.0).astype(BF16)


def _sb_soft(z, mask):
    sp = jnp.maximum(z, 0.0) + jnp.log(1.0 + jnp.exp(-jnp.abs(z)))
    log_beta = z - sp
    if mask is not None:
        sp = jnp.where(mask, sp, 0.0)
    return log_beta, sp


def _sb_weight(log_beta, stick, c, mask):
    w = jnp.exp(log_beta + stick + c)
    if mask is not None:
        w = jnp.where(mask, w, 0.0)
    return w.astype(BF16)


def _sb_tiles(qs, biases, blocks, carry, neg_upper, keys_on_lanes=False):
    tiles = [(h, b) for b in range(len(blocks)) for h in range(len(qs))]
    z = {}
    for h, b in tiles:
        kb = blocks[b][0]
        z[h, b] = (_dot(qs[h], kb) if keys_on_lanes else _dot_nt(qs[h], kb)) + biases[h]
    hi, lo, log_beta, c_at = {}, {}, {}, {}
    c_run = [cr[0] for cr in carry]
    for h, b in tiles:
        log_beta[h, b], sp = _sb_soft(z[h, b], blocks[b][2])
        hi[h, b], lo[h, b] = _split2(sp)
        c_at[h, b] = c_run[h]
        c_run[h] = c_run[h] - jnp.sum(sp, axis=1, keepdims=True)
    stick = {}
    upper2 = jnp.concatenate([neg_upper, neg_upper], axis=0)
    for h, b in tiles:
        stick[h, b] = _dot(jnp.concatenate([hi[h, b], lo[h, b]], axis=1), upper2)
    w = {t: _sb_weight(log_beta[t], stick[t], c_at[t], blocks[t[1]][2]) for t in tiles}
    acc = [cr[1] for cr in carry]
    for h, b in tiles:
        vb = blocks[b][1]
        acc[h] = acc[h] + (_dot_nt(w[h, b], vb) if keys_on_lanes else _dot(w[h, b], vb))
    return tuple((c_run[h], acc[h]) for h in range(len(qs)))


def _sb_prompt_kernel(bias_ref, q_ref, k_ref, v_ref, o_ref, kbf, vbf, *, seq, tq, nqb, nh):
    p = pl.program_id(1)
    qi = pl.program_id(2)
    pad = nqb * tq - seq
    hd = SB_HEAD_DIM
    width = nh * hd

    @pl.when(qi == 0)
    def _():
        kbf[0:seq, :] = k_ref[0].astype(BF16)
        v16 = v_ref[0].astype(BF16)
        vhead = lax.broadcasted_iota(jnp.int32, (seq, width), 1) // hd
        for h in range(nh):
            vbf[h, 0:seq, :] = jnp.where(vhead == h, v16, jnp.zeros_like(v16))
        if pad:
            kbf[seq:seq + pad, :] = jnp.zeros((pad, width), BF16)
            for h in range(nh):
                vbf[h, seq:seq + pad, :] = jnp.zeros((pad, width), BF16)

    def run(rows):
        q = q_ref[0, 0:rows, :] * (hd ** -0.5)
        head = lax.broadcasted_iota(jnp.int32, (rows, width), 1) // hd
        rr = lax.broadcasted_iota(jnp.int32, (rows, tq), 0)
        cc = lax.broadcasted_iota(jnp.int32, (rows, tq), 1)
        diag_mask = cc < rr
        neg_upper = _neg_strict_upper(tq)
        zero = jnp.zeros_like(q)
        q_heads = [jnp.where(head == h, q, zero) for h in range(nh)]
        biases = [bias_ref[nh * p + h] for h in range(nh)]

        def block(j, mask):
            off = pl.multiple_of(j * tq, tq)
            return kbf[pl.ds(off, tq), :], [vbf[h, pl.ds(off, tq), :] for h in range(nh)], mask

        def visit(carry, *blocks):
            cs, acc = list(carry[0]), carry[1]
            tiles = [(h, b) for b in range(len(blocks)) for h in range(nh)]
            z = {(h, b): _dot_nt(q_heads[h], blocks[b][0]) + biases[h] for h, b in tiles}
            sp16, log_beta, c_at = {}, {}, {}
            for h, b in tiles:
                log_beta[h, b], sp = _sb_soft(z[h, b], blocks[b][2])
                sp16[h, b] = sp.astype(BF16)
                c_at[h, b] = cs[h]
                cs[h] = cs[h] - jnp.sum(sp, axis=1, keepdims=True)
            stick = {t: _dot(sp16[t], neg_upper) for t in tiles}
            w = {t: _sb_weight(log_beta[t], stick[t], c_at[t], blocks[t[1]][2]) for t in tiles}
            for h, b in tiles:
                acc = acc + _dot(w[h, b], blocks[b][1][h])
            return tuple(cs), acc

        c0 = (jnp.zeros((rows, 1), F32),) * nh
        carry = visit((c0, jnp.zeros((rows, width), F32)), block(qi, diag_mask))
        carry = lax.cond(qi % 2 == 1, lambda cr: visit(cr, block(qi - 1, None)), lambda cr: cr, carry)
        top = qi - 1 - qi % 2
        carry = lax.fori_loop(
            0, qi // 2, lambda i, cr: visit(cr, block(top - 2 * i, None), block(top - 2 * i - 1, None)), carry)
        o_ref[0, 0:rows, :] = carry[1].astype(o_ref.dtype)

    tail = seq - (nqb - 1) * tq
    if tail == tq:
        run(tq)
    else:
        pl.when(qi < nqb - 1)(lambda: run(tq))
        pl.when(qi == nqb - 1)(lambda: run(tail))


def sb_prompt_attention(q16, k32, v32, sb_bias, name):
    bsz, seq, d = q16.shape
    tq = SB_TQ
    nqb = pl.cdiv(seq, tq)
    assert (seq - (nqb - 1) * tq) % SUB_BF16 == 0
    nh = SB_GROUP
    w2 = nh * SB_HEAD_DIM
    kern = functools.partial(_sb_prompt_kernel, seq=seq, tq=tq, nqb=nqb, nh=nh)
    return pl.pallas_call(
        kern,
        grid_spec=pltpu.PrefetchScalarGridSpec(
            num_scalar_prefetch=1,
            grid=(bsz, d // w2, nqb),
            in_specs=[pl.BlockSpec((1, tq, w2), lambda b, p, i, s: (b, i, p)),
                      pl.BlockSpec((1, seq, w2), lambda b, p, i, s: (b, 0, p)),
                      pl.BlockSpec((1, seq, w2), lambda b, p, i, s: (b, 0, p))],
            out_specs=pl.BlockSpec((1, tq, w2), lambda b, p, i, s: (b, i, p)),
            scratch_shapes=[pltpu.VMEM((nqb * tq, w2), BF16), pltpu.VMEM((nh, nqb * tq, w2), BF16)]),
        out_shape=jax.ShapeDtypeStruct((bsz, seq, d), BF16),
        compiler_params=_params(("arbitrary", "arbitrary", "arbitrary")),
        name=name,
    )(sb_bias, q16, k32, v32)


def _sb_sample_kernel(pt_ref, bias_ref, q_ref, kn_ref, vn_ref, *rest, pps, t_new, page):
    k_pages = rest[:pps]
    v_pages = rest[pps:2 * pps]
    o_ref = rest[2 * pps]
    qbd_ref, bias_mat, c_ref, acc_ref = rest[2 * pps + 1:]
    s = pl.program_id(1)
    hd = SB_HEAD_DIM
    d = SB_HEADS * hd
    rows = SB_HEADS * t_new
    neg_upper = _neg_strict_upper(page)

    def visit(blocks, keys_on_lanes):
        ((c, acc),) = _sb_tiles([qbd_ref[...]], [bias_mat[...]], blocks, [(c_ref[...], acc_ref[...])], neg_upper,
                                keys_on_lanes)
        c_ref[...] = c
        acc_ref[...] = acc

    @pl.when(s == 0)
    def _():
        q = q_ref[0] * (hd ** -0.5)
        qrep = jnp.concatenate([q] * SB_HEADS, axis=0)
        rh = lax.broadcasted_iota(jnp.int32, (rows, d), 0) // t_new
        ch = lax.broadcasted_iota(jnp.int32, (rows, d), 1) // hd
        qbd_ref[...] = jnp.where(rh == ch, qrep, 0.0).astype(BF16)
        rh2 = lax.broadcasted_iota(jnp.int32, (rows, page), 0) // t_new
        bm = jnp.zeros((rows, page), F32)
        for h in range(SB_HEADS):
            bm = jnp.where(rh2 == h, bias_ref[h], bm)
        bias_mat[...] = bm
        c_ref[...] = jnp.zeros_like(c_ref)
        acc_ref[...] = jnp.zeros_like(acc_ref)
        zpad = jnp.zeros((page - t_new, d), F32)
        kn = jnp.concatenate([kn_ref[0], zpad], axis=0).astype(BF16)
        vn = jnp.concatenate([vn_ref[0], zpad], axis=0).astype(BF16)
        tq = lax.broadcasted_iota(jnp.int32, (rows, page), 0) % t_new
        sk = lax.broadcasted_iota(jnp.int32, (rows, page), 1)
        visit([(kn, vn, sk < tq)], False)

    visit([(k_pages[r][0].astype(BF16), v_pages[r][0].astype(BF16), None) for r in range(pps)], True)

    @pl.when(s == pl.num_programs(1) - 1)
    def _():
        acc = acc_ref[...]
        ch = lax.broadcasted_iota(jnp.int32, (t_new, d), 1) // hd
        out = jnp.zeros((t_new, d), F32)
        for h in range(SB_HEADS):
            out = out + jnp.where(ch == h, acc[h * t_new:(h + 1) * t_new, :], 0.0)
        o_ref[0] = out


def sb_sample_attention(q, k_new, v_new, cache_k, cache_v, page_table, sb_bias, name):
    bsz, t_new, d = q.shape
    page = cache_k.shape[2]
    n_pages = page_table.shape[1]
    assert page == LANES and SB_HEADS * t_new == LANES
    pps = _pick(n_pages, (16, 8, 4, 2, 1))
    nstep = n_pages // pps
    pt = page_table.reshape(-1).astype(jnp.int32)

    def page_map(r):
        return lambda b, s, pt_ref, bias_ref: (pt_ref[b * n_pages + (n_pages - 1 - (s * pps + r))], 0, 0)

    tok = pl.BlockSpec((1, t_new, d), lambda b, s, *_: (b, 0, 0))
    page_specs = [pl.BlockSpec((1, d, page), page_map(r)) for r in range(pps)]
    kern = functools.partial(_sb_sample_kernel, pps=pps, t_new=t_new, page=page)
    return pl.pallas_call(
        kern,
        grid_spec=pltpu.PrefetchScalarGridSpec(
            num_scalar_prefetch=2,
            grid=(bsz, nstep),
            in_specs=[tok, tok, tok] + page_specs + page_specs,
            out_specs=tok,
            scratch_shapes=[pltpu.VMEM((LANES, d), BF16), pltpu.VMEM((LANES, page), F32),
                            pltpu.VMEM((LANES, page), F32), pltpu.VMEM((LANES, d), F32)]),
        out_shape=jax.ShapeDtypeStruct((bsz, t_new, d), F32),
        compiler_params=_params(("arbitrary", "arbitrary")),
        name=name,
    )(pt, sb_bias, q, k_new, v_new, *([cache_k] * pps), *([cache_v] * pps))


def _gate_kernel(x_ref, wl_ref, w2_ref, bg_ref, o_ref):
    g_low = _dot(x_ref[...].astype(BF16), wl_ref[...].astype(BF16))
    a_hi, a_lo = _split2(g_low)
    w_hi, w_lo = _split2(w2_ref[...])
    pre = _dot(a_hi, w_hi) + _dot(a_hi, w_lo) + _dot(a_lo, w_hi) + bg_ref[...]
    o_ref[...] = -_softplus(-pre) * (1.0 / GLA_TAU)


def gla_log_gate(x, w_low, w_gate2, b_gate, name):
    m, k = x.shape
    rank, kd = w_gate2.shape
    wl = jnp.pad(w_low, ((0, 0), (0, LANES - rank)))
    w2 = jnp.pad(w_gate2, ((0, LANES - rank), (0, 0)))
    tm = _pick(m, (2064, 1024, 512, 256, 128, 16))
    row = lambda i: (i, 0)
    fix = lambda i: (0, 0)
    return pl.pallas_call(
        _gate_kernel,
        grid=(m // tm,),
        in_specs=[pl.BlockSpec((tm, k), row), pl.BlockSpec((k, LANES), fix), pl.BlockSpec((LANES, kd), fix),
                  pl.BlockSpec((1, kd), fix)],
        out_specs=pl.BlockSpec((tm, kd), row),
        out_shape=jax.ShapeDtypeStruct((m, kd), F32),
        compiler_params=_params(("arbitrary",)),
        name=name,
    )(x, wl, w2, b_gate.reshape(1, kd))


def _gla_chunks(qs, ks, vs, lgs, states):
    heads = range(len(qs))
    c, dk = qs[0].shape
    dv = vs[0].shape[1]
    sub = GLA_SUB
    nsub = c // sub
    ri = lax.broadcasted_iota(jnp.int32, (c, c), 0)
    ci = lax.broadcasted_iota(jnp.int32, (c, c), 1)
    lower = jnp.where(ri >= ci, 1.0, 0.0).astype(BF16)
    ones_c = jnp.ones((c, dk), BF16)
    ones_k = jnp.ones((dk, LANES), BF16)
    row16 = lax.broadcasted_iota(jnp.int32, (sub, dk), 0)
    lane16 = lax.broadcasted_iota(jnp.int32, (sub, LANES), 1)
    rowc = lax.broadcasted_iota(jnp.int32, (c, dk), 0)
    zrows = jnp.zeros((LANES - c, dk), BF16) if c < LANES else None

    parts = [_split3(lgs[h]) for h in heads]
    b = [_dot(lower, parts[h][0]) + _dot(lower, parts[h][1]) + _dot(lower, parts[h][2]) for h in heads]
    b_end_col = [_dot_tn(parts[h][0], ones_c) + _dot_tn(parts[h][1], ones_c) + _dot_tn(parts[h][2], ones_c)
                 for h in heads]

    q_in = [(qs[h] * jnp.exp(b[h])).astype(BF16) for h in heads]
    k_end = [(ks[h] * jnp.exp(b[h][c - 1:c, :] - b[h])).astype(BF16) for h in heads]
    o = [_dot(q_in[h], states[h].astype(BF16)) for h in heads]
    kv = [_dot_tn(k_end[h], vs[h]) for h in heads]
    new_states = [jnp.concatenate([jnp.exp(b_end_col[h])] * (dv // dk), axis=1) * states[h] + kv[h] for h in heads]

    pieces = []
    for h in heads:
        ph = []
        for i in range(nsub):
            qsub = qs[h][i * sub:(i + 1) * sub]
            ksub = ks[h][i * sub:(i + 1) * sub]
            bsub = b[h][i * sub:(i + 1) * sub]
            for s in range(sub):
                e = jnp.exp(jnp.where(row16 >= s, bsub - bsub[s:s + 1, :], -1e30))
                ph.append((qsub * e * ksub[s:s + 1, :]).astype(BF16))
        pieces.append(jnp.concatenate(ph, axis=0))
    pair = [_dot(pieces[h], ones_k) for h in heads]
    qq, kk = {}, {}
    for h in heads:
        for i in range(1, nsub):
            b_ref = b[h][i * sub - 1:i * sub, :]
            qq[h, i] = (qs[h][i * sub:(i + 1) * sub] * jnp.exp(b[h][i * sub:(i + 1) * sub] - b_ref)).astype(BF16)
            kki = jnp.where(rowc < i * sub, ks[h] * jnp.exp(jnp.minimum(b_ref - b[h], 0.0)), 0.0).astype(BF16)
            kk[h, i] = kki if zrows is None else jnp.concatenate([kki, zrows], axis=0)
    cross = {key: _dot_nt(qq[key], kk[key]) for key in qq}
    scores = []
    for h in heads:
        rows = []
        for i in range(nsub):
            sc = jnp.zeros((sub, LANES), F32)
            for s in range(sub):
                blk = pair[h][(i * sub + s) * sub:(i * sub + s + 1) * sub]
                sc = jnp.where(lane16 == i * sub + s, blk, sc)
            if i > 0:
                sc = sc + cross[h, i]
            rows.append(sc)
        scores.append(jnp.concatenate(rows, axis=0).astype(BF16))
    vpad = [vs[h] if c == LANES else jnp.concatenate([vs[h], jnp.zeros((LANES - c, dv), BF16)], axis=0)
            for h in heads]
    o = [o[h] + _dot(scores[h], vpad[h]) for h in heads]
    return o, new_states


def _gla_finish(o, r, gn):
    ms = jnp.mean(o * o, axis=-1, keepdims=True)
    return o * lax.rsqrt(ms + RMS_EPS) * gn * (r * _sigmoid(r))


def _gla_prompt_kernel(q_ref, k_ref, lg_ref, v_ref, r_ref, gn_ref, og_ref, s_ref, state):
    scale = GLA_HEAD_K ** -0.5
    dk, dv, nh = GLA_HEAD_K, GLA_HEAD_V, GLA_HEADS
    ci = pl.program_id(1)

    @pl.when(ci == 0)
    def _():
        state[...] = jnp.zeros_like(state)

    kslice = [slice(h * dk, (h + 1) * dk) for h in range(nh)]
    vslice = [slice(h * dv, (h + 1) * dv) for h in range(nh)]
    o, s_new = _gla_chunks([q_ref[0, :, ks] * scale for ks in kslice], [k_ref[0, :, ks] for ks in kslice],
                           [v_ref[0, :, vs] for vs in vslice], [lg_ref[0, :, ks] for ks in kslice],
                           [state[h] for h in range(nh)])
    gn = gn_ref[...]
    for h in range(nh):
        state[h] = s_new[h]
        og_ref[0, :, vslice[h]] = _gla_finish(o[h], r_ref[0, :, vslice[h]].astype(F32), gn).astype(og_ref.dtype)

    @pl.when(ci == pl.num_programs(1) - 1)
    def _():
        s_ref[0] = state[...]


def gla_prompt(q, k, lg, v16, r16, gnorm_g, name):
    bsz, seq, _ = q.shape
    dk, dv, nh = GLA_HEAD_K, GLA_HEAD_V, GLA_HEADS
    chunk = _pick(seq, GLA_CHUNKS)
    kspec = pl.BlockSpec((1, chunk, nh * dk), lambda b, c: (b, c, 0))
    vspec = pl.BlockSpec((1, chunk, nh * dv), lambda b, c: (b, c, 0))
    return pl.pallas_call(
        _gla_prompt_kernel,
        grid=(bsz, seq // chunk),
        in_specs=[kspec, kspec, kspec, vspec, vspec, pl.BlockSpec((1, dv), lambda b, c: (0, 0))],
        out_specs=[vspec, pl.BlockSpec((1, nh, dk, dv), lambda b, c: (b, 0, 0, 0))],
        out_shape=[jax.ShapeDtypeStruct((bsz, seq, nh * dv), BF16),
                   jax.ShapeDtypeStruct((bsz, nh, dk, dv), F32)],
        scratch_shapes=[pltpu.VMEM((nh, dk, dv), F32)],
        compiler_params=_params(("arbitrary", "arbitrary")),
        name=name,
    )(q, k, lg, v16, r16, gnorm_g.reshape(1, dv))


def _gla_sample_kernel(q_ref, k_ref, lg_ref, v_ref, r_ref, gn_ref, s0_ref, og_ref, s_ref, *, t_new):
    scale = GLA_HEAD_K ** -0.5
    dk, dv, nh = GLA_HEAD_K, GLA_HEAD_V, GLA_HEADS
    padk = jnp.zeros((GLA_SUB - t_new, dk), F32)
    padv = jnp.zeros((GLA_SUB - t_new, dv), F32)
    kslice = [slice(h * dk, (h + 1) * dk) for h in range(nh)]
    vslice = [slice(h * dv, (h + 1) * dv) for h in range(nh)]
    o, s_new = _gla_chunks(
        [jnp.concatenate([q_ref[0, :, ks] * scale, padk], axis=0) for ks in kslice],
        [jnp.concatenate([k_ref[0, :, ks], padk], axis=0) for ks in kslice],
        [jnp.concatenate([v_ref[0, :, vs], padv], axis=0).astype(BF16) for vs in vslice],
        [jnp.concatenate([lg_ref[0, :, ks], padk], axis=0) for ks in kslice],
        [s0_ref[0, h] for h in range(nh)])
    gn = gn_ref[...]
    for h in range(nh):
        s_ref[0, h] = s_new[h]
        og_ref[0, :, vslice[h]] = _gla_finish(o[h][0:t_new], r_ref[0, :, vslice[h]], gn)


def gla_sample(q, k, lg, v, r, gnorm_g, state, name):
    bsz, t_new, _ = q.shape
    dk, dv, nh = GLA_HEAD_K, GLA_HEAD_V, GLA_HEADS
    assert t_new <= GLA_SUB
    kspec = pl.BlockSpec((1, t_new, nh * dk), lambda b: (b, 0, 0))
    vspec = pl.BlockSpec((1, t_new, nh * dv), lambda b: (b, 0, 0))
    sspec = pl.BlockSpec((1, nh, dk, dv), lambda b: (b, 0, 0, 0))
    kern = functools.partial(_gla_sample_kernel, t_new=t_new)
    return pl.pallas_call(
        kern,
        grid=(bsz,),
        in_specs=[kspec, kspec, kspec, vspec, vspec, pl.BlockSpec((1, dv), lambda b: (0, 0)), sspec],
        out_specs=[vspec, sspec],
        out_shape=[jax.ShapeDtypeStruct((bsz, t_new, nh * dv), F32),
                   jax.ShapeDtypeStruct((bsz, nh, dk, dv), F32)],
        compiler_params=_params(("arbitrary",)),
        name=name,
    )(q, k, lg, v, r, gnorm_g.reshape(1, dv), state)


def _router_kernel(h_ref, wr_ref, meta_ref, cnt_ref, carry, *, n_tok, tm):
    i = pl.program_id(0)

    @pl.when(i == 0)
    def _():
        carry[...] = jnp.zeros_like(carry)

    hrow = lax.broadcasted_iota(jnp.int32, h_ref.shape, 0)
    h_hi, h_lo = _split2(jnp.where(i * tm + hrow < n_tok, h_ref[...], 0.0))
    w_hi, w_lo = _split2(wr_ref[...])
    logits = _dot(h_hi, w_hi) + _dot(h_lo, w_hi) + _dot(h_hi, w_lo)
    lane = lax.broadcasted_iota(jnp.int32, (tm, LANES), 1)
    row = lax.broadcasted_iota(jnp.int32, (tm, LANES), 0)
    neg = -jnp.inf
    l1 = jnp.where(lane < N_EXPERTS, logits, neg)
    m1 = jnp.max(l1, axis=1, keepdims=True)
    i1 = jnp.min(jnp.where(l1 == m1, lane, LANES), axis=1, keepdims=True)
    l2 = jnp.where(lane == i1, neg, l1)
    m2 = jnp.max(l2, axis=1, keepdims=True)
    i2 = jnp.min(jnp.where(l2 == m2, lane, LANES), axis=1, keepdims=True)
    e = jnp.exp(m2 - m1)
    w1 = 1.0 / (1.0 + e)
    w2 = e * w1
    valid = (i * tm + row) < n_tok
    hit = jnp.where(valid & ((lane == i1) | (lane == i2)), 1.0, 0.0)
    ri = lax.broadcasted_iota(jnp.int32, (tm, tm), 0)
    ci = lax.broadcasted_iota(jnp.int32, (tm, tm), 1)
    before = jnp.where(ri > ci, 1.0, 0.0).astype(BF16)
    rank = _dot(before, hit.astype(BF16)) + carry[0:1, :]
    r1 = jnp.sum(jnp.where(lane == i1, rank, 0.0), axis=1, keepdims=True)
    r2 = jnp.sum(jnp.where(lane == i2, rank, 0.0), axis=1, keepdims=True)
    carry[0:1, :] = carry[0:1, :] + jnp.sum(hit, axis=0, keepdims=True)
    meta = jnp.where(lane == 0, i1.astype(F32), 0.0)
    meta = jnp.where(lane == 1, i2.astype(F32), meta)
    meta = jnp.where(lane == 2, w1, meta)
    meta = jnp.where(lane == 3, w2, meta)
    meta = jnp.where(lane == 4, r1, meta)
    meta = jnp.where(lane == 5, r2, meta)
    meta_ref[...] = meta

    @pl.when(i == pl.num_programs(0) - 1)
    def _():
        cnt_ref[...] = carry[...]


def moe_route(h32, w_router):
    n_tok, d = h32.shape
    tm = ROUTE_TM
    nt = pl.cdiv(n_tok, tm)
    wr = jnp.pad(w_router, ((0, 0), (0, LANES - w_router.shape[1])))
    kern = functools.partial(_router_kernel, n_tok=n_tok, tm=tm)
    return pl.pallas_call(
        kern,
        grid=(nt,),
        in_specs=[pl.BlockSpec((tm, d), lambda i: (i, 0)), pl.BlockSpec((d, LANES), lambda i: (0, 0))],
        out_specs=[pl.BlockSpec((tm, LANES), lambda i: (i, 0)), pl.BlockSpec((8, LANES), lambda i: (0, 0))],
        out_shape=[jax.ShapeDtypeStruct((nt * tm, LANES), F32), jax.ShapeDtypeStruct((8, LANES), F32)],
        scratch_shapes=[pltpu.VMEM((8, LANES), F32)],
        compiler_params=_params(("arbitrary",)),
        name="moe_route",
    )(h32, wr)


def _row_copy(src, dst, i, j, sem):
    return pltpu.make_async_copy(src.at[pl.ds(i, 1)], dst.at[pl.ds(j, 1)], sem)


def _dispatch_kernel(pos_ref, h_ref, xs_in, xs_out, sem, *, tm):
    del xs_in

    def for_each(fn):
        def body(r, carry):
            for slot in range(2):
                fn(_row_copy(h_ref, xs_out, r, pos_ref[0, 0, 2 * r + slot], sem))
            return carry
        lax.fori_loop(0, tm, body, 0, unroll=DMA_UNROLL)

    for_each(lambda cp: cp.start())
    for_each(lambda cp: cp.wait())


def moe_dispatch(h32, pos, n_rows):
    n_tok, d = h32.shape
    tm = _row_tile(n_tok, 1536)
    nt = n_tok // tm
    pos = pos.reshape(nt, 1, 2 * tm)
    kern = functools.partial(_dispatch_kernel, tm=tm)
    any_spec = pl.BlockSpec(memory_space=pl.ANY)
    return pl.pallas_call(
        kern,
        grid=(nt,),
        in_specs=[pl.BlockSpec((1, 1, 2 * tm), lambda i: (i, 0, 0), memory_space=pltpu.SMEM),
                  pl.BlockSpec((tm, d), lambda i: (i, 0)), any_spec],
        out_specs=any_spec,
        out_shape=jax.ShapeDtypeStruct((n_rows, d), F32),
        scratch_shapes=[pltpu.SemaphoreType.DMA(())],
        input_output_aliases={2: 0},
        compiler_params=_params(("arbitrary",)),
        name="moe_dispatch",
    )(pos, h32, jnp.zeros((n_rows, d), F32))


def _experts_kernel(te_ref, tv_ref, x_ref, wg_ref, wu_ref, wd_ref, y_ref, xbf, acc):
    t = pl.program_id(0)
    f = pl.program_id(1)
    live = tv_ref[t] > 0

    @pl.when(f == 0)
    def _():
        xbf[...] = x_ref[...].astype(BF16)
        acc[...] = jnp.zeros_like(acc)

    @pl.when(live)
    def _():
        x = xbf[...]
        gate = _dot(x, wg_ref[0].astype(BF16))
        up = _dot(x, wu_ref[0].astype(BF16))
        act = (gate * _sigmoid(gate) * up).astype(BF16)
        acc[...] += _dot(act, wd_ref[0].astype(BF16))

    @pl.when(f == pl.num_programs(1) - 1)
    def _():
        y_ref[...] = acc[...]


def moe_experts(xs, w_gu_e, w_down_e, tile_expert, tile_live):
    n_rows, d = xs.shape
    dff = w_down_e.shape[1]
    tm, tf = MOE_TM, MOE_TF
    nf = dff // tf
    assert nf * tf == dff and n_rows % tm == 0

    def fcol(t, f, tv):
        return jnp.where(tv[t] > 0, f, nf - 1)

    return pl.pallas_call(
        _experts_kernel,
        grid_spec=pltpu.PrefetchScalarGridSpec(
            num_scalar_prefetch=2,
            grid=(n_rows // tm, nf),
            in_specs=[pl.BlockSpec((tm, d), lambda t, f, te, tv: (t, 0)),
                      pl.BlockSpec((1, d, tf), lambda t, f, te, tv: (te[t], 0, fcol(t, f, tv))),
                      pl.BlockSpec((1, d, tf), lambda t, f, te, tv: (te[t], 0, nf + fcol(t, f, tv))),
                      pl.BlockSpec((1, tf, d), lambda t, f, te, tv: (te[t], fcol(t, f, tv), 0))],
            out_specs=pl.BlockSpec((tm, d), lambda t, f, te, tv: (t, 0)),
            scratch_shapes=[pltpu.VMEM((tm, d), BF16), pltpu.VMEM((tm, d), F32)]),
        out_shape=jax.ShapeDtypeStruct((n_rows, d), F32),
        compiler_params=_params(("arbitrary", "arbitrary")),
        name="moe_experts",
    )(tile_expert, tile_live, xs, w_gu_e, w_gu_e, w_down_e)


def _combine_kernel(pos_ref, meta_ref, res_ref, g_ref, b_ref, y_hbm, o32_ref, o16_ref, y1, y2, sem, *, tm):
    def for_each(fn):
        def body(r, carry):
            fn(_row_copy(y_hbm, y1, pos_ref[0, 0, 2 * r], r, sem))
            fn(_row_copy(y_hbm, y2, pos_ref[0, 0, 2 * r + 1], r, sem))
            return carry
        lax.fori_loop(0, tm, body, 0, unroll=DMA_UNROLL)

    for_each(lambda cp: cp.start())
    for_each(lambda cp: cp.wait())
    meta = meta_ref[...]
    mix = meta[:, 2:3] * y1[...] + meta[:, 3:4] * y2[...]
    h = _layer_norm(DEEPNORM_ALPHA * res_ref[...] + mix, g_ref[...], b_ref[...])
    o32_ref[...] = h
    o16_ref[...] = h.astype(BF16)


def moe_combine_deepnorm(ys, pos, meta, res, g, b, name):
    m, d = res.shape
    tm = _pick(m, (688, 512, 256, 128, 16))
    nt = m // tm
    row = lambda i: (i, 0)
    fix = lambda i: (0, 0)
    kern = functools.partial(_combine_kernel, tm=tm)
    return pl.pallas_call(
        kern,
        grid=(nt,),
        in_specs=[pl.BlockSpec((1, 1, 2 * tm), lambda i: (i, 0, 0), memory_space=pltpu.SMEM),
                  pl.BlockSpec((tm, LANES), row), pl.BlockSpec((tm, d), row),
                  pl.BlockSpec((1, d), fix), pl.BlockSpec((1, d), fix), pl.BlockSpec(memory_space=pl.ANY)],
        out_specs=[pl.BlockSpec((tm, d), row), pl.BlockSpec((tm, d), row)],
        out_shape=[jax.ShapeDtypeStruct((m, d), F32), jax.ShapeDtypeStruct((m, d), BF16)],
        scratch_shapes=[pltpu.VMEM((tm, d), F32), pltpu.VMEM((tm, d), F32), pltpu.SemaphoreType.DMA(())],
        compiler_params=_params(("arbitrary",)),
        name=name,
    )(pos.reshape(nt, 1, 2 * tm), meta, res, g.reshape(1, d), b.reshape(1, d), ys)


def moe_deepnorm(hp32, hs32, w_router, w_gu_e, w_down_e, g, b, layer):
    mp = hp32.shape[0]
    h_all = jnp.concatenate([hp32, hs32], axis=0)
    n_tok = h_all.shape[0]
    meta, cnt = moe_route(h_all, w_router)
    meta = meta[:n_tok]
    counts = cnt[0, :N_EXPERTS].astype(jnp.int32)
    padded = ((counts + MOE_TM - 1) // MOE_TM) * MOE_TM
    ends = jnp.cumsum(padded)
    starts = ends - padded
    e1 = meta[:, 0].astype(jnp.int32)
    e2 = meta[:, 1].astype(jnp.int32)
    eids = jnp.arange(N_EXPERTS, dtype=jnp.int32)
    start1 = jnp.sum(jnp.where(e1[:, None] == eids[None, :], starts[None, :], 0), axis=1)
    start2 = jnp.sum(jnp.where(e2[:, None] == eids[None, :], starts[None, :], 0), axis=1)
    pos = jnp.stack([start1 + meta[:, 4].astype(jnp.int32), start2 + meta[:, 5].astype(jnp.int32)], axis=1)
    n_tiles = pl.cdiv(2 * n_tok, MOE_TM) + N_EXPERTS
    tile_start = jnp.arange(n_tiles, dtype=jnp.int32) * MOE_TM
    tile_live = (tile_start < ends[-1]).astype(jnp.int32)
    tile_expert = jnp.sum((tile_start[:, None] >= ends[None, :]).astype(jnp.int32), axis=1)
    last_live = jnp.sum((jnp.maximum(ends[-1] - 1, 0) >= ends).astype(jnp.int32))
    tile_expert = jnp.where(tile_live > 0, tile_expert, last_live).astype(jnp.int32)

    xs = moe_dispatch(h_all, pos, n_tiles * MOE_TM)
    ys = moe_experts(xs, w_gu_e, w_down_e, tile_expert, tile_live)
    outp = moe_combine_deepnorm(ys, pos[:mp], meta[:mp], hp32, g, b, f"l{layer}_moe_combine_p")
    outs = moe_combine_deepnorm(ys, pos[mp:], meta[mp:], hs32, g, b, f"l{layer}_moe_combine_s")
    return outp, outs


def _sb_layer(i, hp32, hp16, hs32, hs16, bsz, seq, dec_b, dec_t, cache_k, cache_v, page_table,
              w_qkv, w_o, sb_bias, ln1_g, ln1_b):
    d = D_MODEL
    tag = f"l{i}_"
    qp = linear(hp16, w_qkv, 0, d, BF16, tag + "q_p")
    kp = linear(hp16, w_qkv, d, d, F32, tag + "k_p")
    vp = linear(hp16, w_qkv, 2 * d, d, F32, tag + "v_p")
    qs = linear(hs16, w_qkv, 0, d, F32, tag + "q_s")
    ks = linear(hs16, w_qkv, d, d, F32, tag + "k_s")
    vs = linear(hs16, w_qkv, 2 * d, d, F32, tag + "v_s")
    op = sb_prompt_attention(qp.reshape(bsz, seq, d), kp.reshape(bsz, seq, d), vp.reshape(bsz, seq, d),
                             sb_bias, tag + "sb_p")
    pool, page = cache_k.shape[0], cache_k.shape[1]
    ck = jnp.transpose(cache_k, (0, 2, 3, 1)).reshape(pool, d, page)
    cv = jnp.transpose(cache_v, (0, 2, 3, 1)).reshape(pool, d, page)
    os_ = sb_sample_attention(qs.reshape(dec_b, dec_t, d), ks.reshape(dec_b, dec_t, d), vs.reshape(dec_b, dec_t, d),
                              ck, cv, page_table, sb_bias, tag + "sb_s")
    hp32, hp16 = linear_deepnorm(op.reshape(bsz * seq, d), w_o, hp32, ln1_g, ln1_b, tag + "o_p")
    hs32, hs16 = linear_deepnorm(os_.reshape(dec_b * dec_t, d), w_o, hs32, ln1_g, ln1_b, tag + "o_s")
    shp = (bsz, seq, SB_HEADS, SB_HEAD_DIM)
    shs = (dec_b, dec_t, SB_HEADS, SB_HEAD_DIM)
    new_state = [kp.reshape(shp), vp.reshape(shp), ks.reshape(shs), vs.reshape(shs)]
    return hp32, hp16, hs32, hs16, new_state


def _gla_layer(i, hp32, hp16, hs32, hs16, bsz, seq, dec_b, dec_t, state,
               w_in, w_gate2, b_gate, gnorm_g, w_o, ln1_g, ln1_b):
    tag = f"l{i}_"
    kd, vd = GLA_KEY_DIM, GLA_HEADS * GLA_HEAD_V
    w_low = w_in[:, 2 * kd + 2 * vd:]

    def project(x16, vdtype, sfx):
        q = linear(x16, w_in, 0, kd, F32, tag + "q" + sfx)
        k = linear(x16, w_in, kd, kd, F32, tag + "k" + sfx)
        v = linear(x16, w_in, 2 * kd, vd, vdtype, tag + "v" + sfx)
        r = linear(x16, w_in, 2 * kd + vd, vd, vdtype, tag + "r" + sfx)
        lg = gla_log_gate(x16, w_low, w_gate2, b_gate, tag + "gate" + sfx)
        return q, k, lg, v, r

    q, k, lg, v, r = project(hp16, BF16, "_p")
    og_p, sp = gla_prompt(q.reshape(bsz, seq, kd), k.reshape(bsz, seq, kd), lg.reshape(bsz, seq, kd),
                          v.reshape(bsz, seq, vd), r.reshape(bsz, seq, vd), gnorm_g, tag + "gla_p")
    q, k, lg, v, r = project(hs16, F32, "_s")
    og_s, ss = gla_sample(q.reshape(dec_b, dec_t, kd), k.reshape(dec_b, dec_t, kd), lg.reshape(dec_b, dec_t, kd),
                          v.reshape(dec_b, dec_t, vd), r.reshape(dec_b, dec_t, vd), gnorm_g, state, tag + "gla_s")
    hp32, hp16 = linear_deepnorm(og_p.reshape(bsz * seq, vd), w_o, hp32, ln1_g, ln1_b, tag + "o_p")
    hs32, hs16 = linear_deepnorm(og_s.reshape(dec_b * dec_t, vd), w_o, hs32, ln1_g, ln1_b, tag + "o_s")
    return hp32, hp16, hs32, hs16, [sp, ss]


def kernel(x_prompt, x_sample, cache_k0, cache_v0, state_s1, cache_k2, cache_v2, state_s3, page_table, meta_tokens, l0_w_qkv, l0_w_o, l0_sb_bias, l0_ln1_g, l0_ln1_b, l0_w_gu, l0_w_down, l0_ln2_g, l0_ln2_b, l1_w_in, l1_w_gate2, l1_b_gate, l1_gnorm_g, l1_w_o, l1_ln1_g, l1_ln1_b, l1_w_router, l1_w_gu_e, l1_w_down_e, l1_ln2_g, l1_ln2_b, l2_w_qkv, l2_w_o, l2_sb_bias, l2_ln1_g, l2_ln1_b, l2_w_gu, l2_w_down, l2_ln2_g, l2_ln2_b, l3_w_in, l3_w_gate2, l3_b_gate, l3_gnorm_g, l3_w_o, l3_ln1_g, l3_ln1_b, l3_w_router, l3_w_gu_e, l3_w_down_e, l3_ln2_g, l3_ln2_b):
    bsz, seq0, d = x_prompt.shape
    dec_b, dec_t, _ = x_sample.shape
    n_meta = meta_tokens.shape[0]
    seq = n_meta + seq0
    meta = jnp.broadcast_to(meta_tokens.astype(x_prompt.dtype)[None], (bsz, n_meta, d))
    hp32 = jnp.concatenate([meta, x_prompt], axis=1).reshape(bsz * seq, d)
    hs32 = x_sample.reshape(dec_b * dec_t, d)
    hp16 = hp32.astype(BF16)
    hs16 = hs32.astype(BF16)

    sb = [(cache_k0, cache_v0, l0_w_qkv, l0_w_o, l0_sb_bias, l0_ln1_g, l0_ln1_b),
          (cache_k2, cache_v2, l2_w_qkv, l2_w_o, l2_sb_bias, l2_ln1_g, l2_ln1_b)]
    ffn = [(l0_w_gu, l0_w_down, l0_ln2_g, l0_ln2_b), (l2_w_gu, l2_w_down, l2_ln2_g, l2_ln2_b)]
    gla = [(state_s1, l1_w_in, l1_w_gate2, l1_b_gate, l1_gnorm_g, l1_w_o, l1_ln1_g, l1_ln1_b),
           (state_s3, l3_w_in, l3_w_gate2, l3_b_gate, l3_gnorm_g, l3_w_o, l3_ln1_g, l3_ln1_b)]
    moe = [(l1_w_router, l1_w_gu_e, l1_w_down_e, l1_ln2_g, l1_ln2_b),
           (l3_w_router, l3_w_gu_e, l3_w_down_e, l3_ln2_g, l3_ln2_b)]

    new_state = []
    for i in range(DEPTH):
        if i % 2 == 0:
            cache_k, cache_v, w_qkv, w_o, sb_bias, g1, b1 = sb[i // 2]
            hp32, hp16, hs32, hs16, st = _sb_layer(i, hp32, hp16, hs32, hs16, bsz, seq, dec_b, dec_t,
                                                   cache_k, cache_v, page_table, w_qkv, w_o, sb_bias, g1, b1)
            w_gu, w_down, g2, b2 = ffn[i // 2]
            hp32, hp16 = swiglu_deepnorm(hp16, w_gu, w_down, hp32, g2, b2, f"l{i}_ffn_p")
            hs32, hs16 = swiglu_deepnorm(hs16, w_gu, w_down, hs32, g2, b2, f"l{i}_ffn_s")
        else:
            state, w_in, w_gate2, b_gate, gnorm_g, w_o, g1, b1 = gla[i // 2]
            hp32, hp16, hs32, hs16, st = _gla_layer(i, hp32, hp16, hs32, hs16, bsz, seq, dec_b, dec_t, state,
                                                    w_in, w_gate2, b_gate, gnorm_g, w_o, g1, b1)
            w_router, w_gu_e, w_down_e, g2, b2 = moe[i // 2]
            (hp32, hp16), (hs32, hs16) = moe_deepnorm(hp32, hs32, w_router, w_gu_e, w_down_e, g2, b2, i)
        new_state += st

    y_prompt = hp32.reshape(bsz, seq, d)[:, n_meta:]
    y_sample = hs32.reshape(dec_b, dec_t, d)
    return (y_prompt, y_sample, *new_state)
```
